```python
import math
import jax, jax.numpy as jnp
from jax import lax
import numpy as np

D_MODEL = 1024
BATCH = 8
SEQ = 4096
DEPTH = 2

GRID_W = 64
N_Q_HEADS = 16
N_KV_HEADS = 4
HEAD_DIM = 64
ATTN_WIDTH = N_Q_HEADS * HEAD_DIM
KV_WIDTH = N_KV_HEADS * HEAD_DIM
ROPE_THETA = 10000.0
Q_BLOCK = 128
SSM_EXPAND = 2
D_INNER = SSM_EXPAND * D_MODEL
SSM_HEAD_DIM = 64
N_SSM_HEADS = D_INNER // SSM_HEAD_DIM
N_SSM_GROUPS = 4
D_STATE = 128
D_CONV = 5
CHUNK = 128
CONV_DIM = D_INNER + 2 * N_SSM_GROUPS * D_STATE
MIX_WIDTH = ATTN_WIDTH + D_INNER
IN_PROJ_WIDTH = ATTN_WIDTH + 2 * KV_WIDTH + D_INNER + CONV_DIM + 2 * N_SSM_HEADS
D_FF = 4 * D_MODEL
NORM_EPS = 1e-5
QK_EPS = 1e-6

kernel_name = "hybrid_parallel_ssd_axial_gqa_encoder"


def rmsnorm(x, w, eps=NORM_EPS):
    xf = x.astype(jnp.float32)
    y = xf * lax.rsqrt(jnp.mean(xf * xf, axis=-1, keepdims=True) + eps)
    return (y * w.astype(jnp.float32)).astype(x.dtype)


def axial_rope_tables(seq):
    rows = seq // GRID_W
    row_ids = jnp.repeat(jnp.arange(rows, dtype=jnp.int32), GRID_W)
    col_ids = jnp.tile(jnp.arange(GRID_W, dtype=jnp.int32), rows)
    half = HEAD_DIM // 2
    inv_freq = ROPE_THETA ** (-jnp.arange(0, half, 2, dtype=jnp.float32) / half)

    def table(pos):
        ang = pos.astype(jnp.float32)[:, None] * inv_freq[None, :]
        ang = jnp.concatenate([ang, ang], axis=-1)
        return jnp.cos(ang), jnp.sin(ang)

    cr, sr = table(row_ids)
    cc, sc = table(col_ids)
    return cr, sr, cc, sc


def _rotate(x, cos, sin):
    x1, x2 = jnp.split(x, 2, axis=-1)
    rot = jnp.concatenate([-x2, x1], axis=-1)
    return x * cos[None, :, None, :] + rot * sin[None, :, None, :]


def axial_rope(x, tabs):
    cr, sr, cc, sc = tabs
    half = HEAD_DIM // 2
    xf = x.astype(jnp.float32)
    out = jnp.concatenate([_rotate(xf[..., :half], cr, sr), _rotate(xf[..., half:], cc, sc)], axis=-1)
    return out.astype(x.dtype)


def attention_group(q, k, v, q_norm_w, k_norm_w, tabs):
    b, s = q.shape[:2]
    rep = N_Q_HEADS // N_KV_HEADS
    q = axial_rope(rmsnorm(q.reshape(b, s, N_Q_HEADS, HEAD_DIM), q_norm_w, QK_EPS), tabs)
    k = axial_rope(rmsnorm(k.reshape(b, s, N_KV_HEADS, HEAD_DIM), k_norm_w, QK_EPS), tabs)
    v = v.reshape(b, s, N_KV_HEADS, HEAD_DIM)
    nb = s // Q_BLOCK
    qb = q.reshape(b, nb, Q_BLOCK, N_KV_HEADS, rep, HEAD_DIM).transpose(1, 0, 2, 3, 4, 5)
    scale = HEAD_DIM ** -0.5

    def one_block(qblk):
        sc = jnp.einsum("bqgrd,bkgd->bgrqk", qblk, k).astype(jnp.float32) * scale
        p = jax.nn.softmax(sc, axis=-1).astype(v.dtype)
        return jnp.einsum("bgrqk,bkgd->bqgrd", p, v)

    out = lax.map(one_block, qb)
    return out.transpose(1, 0, 2, 3, 4, 5).reshape(b, s, ATTN_WIDTH)


def centred_dwconv(u, w, bias):
    pad = D_CONV // 2
    out = lax.conv_general_dilated(
        u, w[:, None, :].astype(u.dtype), window_strides=(1,), padding=[(pad, pad)],
        dimension_numbers=("NWC", "WIO", "NWC"), feature_group_count=u.shape[-1])
    return out + bias.astype(u.dtype)


def ssd_chunked(xh, a, bm, cm):
    bsz, s, h, p = xh.shape
    g, n = bm.shape[2], bm.shape[3]
    e = h // g
    c = s // CHUNK
    x = xh.reshape(bsz, c, CHUNK, g, e, p)
    a = a.reshape(bsz, c, CHUNK, g, e).transpose(0, 3, 4, 1, 2)
    B = bm.reshape(bsz, c, CHUNK, g, n)
    C = cm.reshape(bsz, c, CHUNK, g, n)
    a_cs = jnp.cumsum(a, axis=-1)
    seg = a_cs[..., :, None] - a_cs[..., None, :]
    mask = jnp.tril(jnp.ones((CHUNK, CHUNK), dtype=bool))
    lmat = jnp.exp(jnp.where(mask, seg, -jnp.inf))
    cb = jnp.einsum("bclgn,bcsgn->bgcls", C, B)
    y_diag = jnp.einsum("bgcls,bgecls,bcsgep->bclgep", cb, lmat, x)
    decay_states = jnp.exp(a_cs[..., -1:] - a_cs)
    states = jnp.einsum("bclgn,bgecl,bclgep->bcgepn", B, decay_states, x)
    chunk_decay = jnp.moveaxis(jnp.exp(a_cs[..., -1]), 3, 0)

    def step(carry, inp):
        st, dec = inp
        return carry * dec[..., None, None] + st, carry

    init = jnp.zeros((bsz, g, e, p, n), dtype=jnp.float32)
    _, prev = lax.scan(step, init, (jnp.moveaxis(states, 1, 0), chunk_decay))
    y_off = jnp.einsum("bclgn,cbgepn,bgecl->bclgep", C, prev, jnp.exp(a_cs))
    return (y_diag + y_off).reshape(bsz, s, h, p)


def ssd_group(z, xbc, dt_raw, conv_w, conv_b, dt_bias_f, dt_bias_b, a_log_f, a_log_b, d_skip, norm_w):
    bsz, s = z.shape[:2]
    f32 = jnp.float32
    xbc = jax.nn.silu(centred_dwconv(xbc, conv_w, conv_b)).astype(f32)
    gn = N_SSM_GROUPS * D_STATE
    xs = xbc[..., :D_INNER].reshape(bsz, s, N_SSM_HEADS, SSM_HEAD_DIM)
    bm = xbc[..., D_INNER:D_INNER + gn].reshape(bsz, s, N_SSM_GROUPS, D_STATE)
    cm = xbc[..., D_INNER + gn:].reshape(bsz, s, N_SSM_GROUPS, D_STATE)
    dt_raw = dt_raw.astype(f32)
    dt_f = jax.nn.softplus(dt_raw[..., :N_SSM_HEADS] + dt_bias_f.astype(f32))
    dt_b = jax.nn.softplus(dt_raw[..., N_SSM_HEADS:] + dt_bias_b.astype(f32))
    A_f = -jnp.exp(a_log_f.astype(f32))
    A_b = -jnp.exp(a_log_b.astype(f32))
    flip = lambda t: jnp.flip(t, axis=1)
    y_f = ssd_chunked(xs * dt_f[..., None], dt_f * A_f, bm, cm)
    y_b = flip(ssd_chunked(flip(xs * dt_b[..., None]), flip(dt_b * A_b), flip(bm), flip(cm)))
    y = y_f + y_b + d_skip.astype(f32)[:, None] * xs
    y = y.reshape(bsz, s, D_INNER) * jax.nn.silu(z.astype(f32))
    yg = y.reshape(bsz, s, N_SSM_GROUPS, D_INNER // N_SSM_GROUPS)
    yg = yg * lax.rsqrt(jnp.mean(yg * yg, axis=-1, keepdims=True) + NORM_EPS)
    y = yg.reshape(bsz, s, D_INNER) * norm_w.astype(f32)
    return y.astype(z.dtype)


def hybrid_layer(x, ln1_w, w_in, conv_w, conv_b, dt_bias_f, dt_bias_b, a_log_f, a_log_b, d_skip,
                 ssm_norm_w, q_norm_w, k_norm_w, w_out, ln2_w, w_up, w_down, tabs):
    h = rmsnorm(x, ln1_w)
    proj = h @ w_in
    i0 = ATTN_WIDTH
    i1 = i0 + KV_WIDTH
    i2 = i1 + KV_WIDTH
    i3 = i2 + D_INNER
    i4 = i3 + CONV_DIM
    q, k, v, z, xbc, dt_raw = jnp.split(proj, [i0, i1, i2, i3, i4], axis=-1)
    attn = attention_group(q, k, v, q_norm_w, k_norm_w, tabs)
    ssm = ssd_group(z, xbc, dt_raw, conv_w, conv_b, dt_bias_f, dt_bias_b, a_log_f, a_log_b,
                    d_skip, ssm_norm_w)
    x = x + jnp.concatenate([attn, ssm], axis=-1) @ w_out
    h = rmsnorm(x, ln2_w)
    x = x + jnp.square(jax.nn.relu(h @ w_up)) @ w_down
    return x


def setup_inputs(seed: int = 0) -> dict:
    key = jax.random.key(seed)
    ks = jax.random.split(key, 20)
    f32 = jnp.float32
    L = DEPTH
    nrm = lambda k, shape, scale: jax.random.normal(k, shape, f32) * scale
    gain = lambda k, shape: 1.0 + 0.02 * jax.random.normal(k, shape, f32)
    dt_lo, dt_hi = 1e-3, 1e-1

    def dt_bias_init(k):
        u = jax.random.uniform(k, (L, N_SSM_HEADS), f32)
        dt = jnp.exp(u * (math.log(dt_hi) - math.log(dt_lo)) + math.log(dt_lo))
        return dt + jnp.log(-jnp.expm1(-dt))

    def a_log_init(k):
        return jnp.log(jax.random.uniform(k, (L, N_SSM_HEADS), f32, 1.0, 16.0))

    return {
        "x": jax.random.normal(ks[0], (BATCH, SEQ, D_MODEL), f32),
        "ln1_w": gain(ks[1], (L, D_MODEL)),
        "w_in": nrm(ks[2], (L, D_MODEL, IN_PROJ_WIDTH), D_MODEL ** -0.5),
        "conv_w": nrm(ks[3], (L, D_CONV, CONV_DIM), D_CONV ** -0.5),
        "conv_b": nrm(ks[4], (L, CONV_DIM), 0.02),
        "dt_bias_fwd": dt_bias_init(ks[5]),
        "dt_bias_bwd": dt_bias_init(ks[6]),
        "a_log_fwd": a_log_init(ks[7]),
        "a_log_bwd": a_log_init(ks[8]),
        "d_skip": gain(ks[9], (L, N_SSM_HEADS)),
        "ssm_norm_w": gain(ks[10], (L, D_INNER)),
        "q_norm_w": gain(ks[11], (L, HEAD_DIM)),
        "k_norm_w": gain(ks[12], (L, HEAD_DIM)),
        "w_out": nrm(ks[13], (L, MIX_WIDTH, D_MODEL), MIX_WIDTH ** -0.5),
        "ln2_w": gain(ks[14], (L, D_MODEL)),
        "w_up": nrm(ks[15], (L, D_MODEL, D_FF), D_MODEL ** -0.5),
        "w_down": nrm(ks[16], (L, D_FF, D_MODEL), D_FF ** -0.5),
        "final_norm_w": gain(ks[17], (D_MODEL,)),
    }


def reference(x, ln1_w, w_in, conv_w, conv_b, dt_bias_fwd, dt_bias_bwd, a_log_fwd, a_log_bwd,
              d_skip, ssm_norm_w, q_norm_w, k_norm_w, w_out, ln2_w, w_up, w_down, final_norm_w):
    tabs = axial_rope_tables(x.shape[1])
    for i in range(DEPTH):
        x = hybrid_layer(x, ln1_w[i], w_in[i], conv_w[i], conv_b[i], dt_bias_fwd[i], dt_bias_bwd[i],
                         a_log_fwd[i], a_log_bwd[i], d_skip[i], ssm_norm_w[i], q_norm_w[i],
                         k_norm_w[i], w_out[i], ln2_w[i], w_up[i], w_down[i], tabs)
    return rmsnorm(x, final_norm_w)
```

```python
import functools

import numpy as np
import jax
import jax.numpy as jnp
from jax import lax
from jax.experimental import pallas as pl
from jax.experimental.pallas import tpu as pltpu

F32 = jnp.float32
BF16 = jnp.bfloat16

D_MODEL = 1024
GRID_W = 64
N_Q_HEADS = 16
N_KV_HEADS = 4
HEAD_DIM = 64
REP = N_Q_HEADS // N_KV_HEADS
ATTN_WIDTH = N_Q_HEADS * HEAD_DIM
KV_WIDTH = N_KV_HEADS * HEAD_DIM
ROPE_THETA = 10000.0
D_INNER = 2048
SSM_HEAD_DIM = 64
N_SSM_HEADS = D_INNER // SSM_HEAD_DIM
N_SSM_GROUPS = 4
HEADS_PER_GROUP = N_SSM_HEADS // N_SSM_GROUPS
GROUP_WIDTH = D_INNER // N_SSM_GROUPS
D_STATE = 128
D_CONV = 5
CHUNK = 128
CONV_DIM = D_INNER + 2 * N_SSM_GROUPS * D_STATE
D_FF = 4 * D_MODEL
NORM_EPS = 1e-5
QK_EPS = 1e-6
LOG2E = 1.4426950408889634

LANES = 128
VMEM_LIMIT_BYTES = 58 * 1024 * 1024

QKV_WIDTH = ATTN_WIDTH + 2 * KV_WIDTH
Z_OFF = QKV_WIDTH
XBC_OFF = Z_OFF + D_INNER
DT_OFF = XBC_OFF + CONV_DIM
DT_PAD = N_SSM_GROUPS * LANES
W_ALL_WIDTH = DT_OFF + DT_PAD
DT_ROWS = 2 * HEADS_PER_GROUP

TM_PROJ = 512
TS_PREP = 512
TQ = 128
KEY_CHUNK = 512
TM_MLP = 256
EPI_ROWS = 256
HALO = 16


def _dot(a, b):
    return jnp.dot(a, b, preferred_element_type=F32)


def _dot_nt(a, b):
    return lax.dot_general(a, b, (((1,), (1,)), ((), ())), preferred_element_type=F32)


def _params(semantics):
    return pltpu.CompilerParams(dimension_semantics=semantics, vmem_limit_bytes=VMEM_LIMIT_BYTES)


def _const_spec(shape):
    nd = len(shape)
    return pl.BlockSpec(shape, lambda *_: (0,) * nd, pipeline_mode=pl.Buffered(1))


def _inproj_kernel(x_ref, ln_ref, w_ref, wdt_t_ref, qkv_ref, z_ref, xbc_ref, dt_ref, dtt_ref):
    x = x_ref[0]
    ms = jnp.mean(x * x, axis=-1, keepdims=True)
    hn = (x * lax.rsqrt(ms + NORM_EPS) * ln_ref[...]).astype(BF16)
    qkv_ref[0] = _dot(hn, w_ref[:, 0:QKV_WIDTH]).astype(BF16)
    z_ref[0] = _dot(hn, w_ref[:, Z_OFF:XBC_OFF]).astype(BF16)
    xbc_ref[0] = _dot(hn, w_ref[:, XBC_OFF:DT_OFF]).astype(BF16)
    dt_ref[0] = _dot(hn, w_ref[:, DT_OFF:W_ALL_WIDTH])
    dtt_ref[0] = _dot_nt(wdt_t_ref[...], hn)


def _in_proj(x, ln_w, w_all, wdt_t):
    b, s, _ = x.shape
    tm = min(TM_PROJ, s)
    grid = (b, s // tm)
    tok = lambda width: pl.BlockSpec((1, tm, width), lambda i, j: (i, j, 0))
    return pl.pallas_call(
        _inproj_kernel,
        grid=grid,
        in_specs=[
            tok(D_MODEL),
            _const_spec((1, D_MODEL)),
            _const_spec((D_MODEL, W_ALL_WIDTH)),
            _const_spec((N_SSM_GROUPS * DT_ROWS, D_MODEL)),
        ],
        out_specs=[
            tok(QKV_WIDTH), tok(D_INNER), tok(CONV_DIM), tok(DT_PAD),
            pl.BlockSpec((1, N_SSM_GROUPS * DT_ROWS, tm), lambda i, j: (i, 0, j)),
        ],
        out_shape=[
            jax.ShapeDtypeStruct((b, s, QKV_WIDTH), BF16),
            jax.ShapeDtypeStruct((b, s, D_INNER), BF16),
            jax.ShapeDtypeStruct((b, s, CONV_DIM), BF16),
            jax.ShapeDtypeStruct((b, s, DT_PAD), F32),
            jax.ShapeDtypeStruct((b, N_SSM_GROUPS * DT_ROWS, s), F32),
        ],
        compiler_params=_params(("parallel", "parallel")),
        name="in_proj",
    )(x, ln_w, w_all, wdt_t)


def _norm_rope(t, w, seg_ones, cos, sin_signed, post_scale):
    width = t.shape[1]
    ss = _dot((t * t).astype(BF16), seg_ones) * (1.0 / HEAD_DIM)
    y = t * lax.rsqrt(ss + QK_EPS) * w
    reps = width // LANES
    cosw = jnp.concatenate([cos] * reps, axis=1)
    sinw = jnp.concatenate([sin_signed] * reps, axis=1)
    quarter = HEAD_DIM // 4
    upper = pltpu.roll(y, width - quarter, 1)
    lower = pltpu.roll(y, quarter, 1)
    lane = lax.broadcasted_iota(jnp.int32, y.shape, 1)
    first_half = (lane % (2 * quarter)) < quarter
    rot = jnp.where(first_half, upper, lower)
    return (y * cosw + rot * sinw) * post_scale


def _qkprep_kernel(q_ref, k_ref, v_ref, cos_ref, sin_ref, qw_ref, kw_ref, seg_ref, rep_ref,
                   qn_ref, kt_ref, vd_ref):
    cos = cos_ref[...]
    sin = sin_ref[...]
    q = _norm_rope(q_ref[0].astype(F32), qw_ref[...], seg_ref[...], cos, sin,
                   (HEAD_DIM ** -0.5) * LOG2E)
    qn_ref[0] = q.astype(BF16)
    k = _norm_rope(k_ref[0].astype(F32), kw_ref[...], seg_ref[0:KV_WIDTH, 0:KV_WIDTH], cos, sin, 1.0)
    kt = k.T
    vrep = _dot(v_ref[0], rep_ref[...])
    for g in range(N_KV_HEADS):
        kg = kt[g * HEAD_DIM:(g + 1) * HEAD_DIM, :]
        kt_ref[0, g] = jnp.concatenate([kg] * REP, axis=0).astype(BF16)
        vd_ref[0, g] = vrep[:, g * REP * HEAD_DIM:(g + 1) * REP * HEAD_DIM].astype(BF16)


def _qk_prep(qkv, cos_t, sin_t, qw, kw, seg_ones, rep_mat):
    b, s, _ = qkv.shape
    ts = min(TS_PREP, s)
    grid = (b, s // ts)
    gw = REP * HEAD_DIM
    return pl.pallas_call(
        _qkprep_kernel,
        grid=grid,
        in_specs=[
            pl.BlockSpec((1, ts, ATTN_WIDTH), lambda i, j: (i, j, 0)),
            pl.BlockSpec((1, ts, KV_WIDTH), lambda i, j: (i, j, ATTN_WIDTH // KV_WIDTH)),
            pl.BlockSpec((1, ts, KV_WIDTH), lambda i, j: (i, j, ATTN_WIDTH // KV_WIDTH + 1)),
            pl.BlockSpec((ts, LANES), lambda i, j: (j, 0)),
            pl.BlockSpec((ts, LANES), lambda i, j: (j, 0)),
            _const_spec((1, ATTN_WIDTH)),
            _const_spec((1, KV_WIDTH)),
            _const_spec((ATTN_WIDTH, ATTN_WIDTH)),
            _const_spec((KV_WIDTH, N_KV_HEADS * gw)),
        ],
        out_specs=[
            pl.BlockSpec((1, ts, ATTN_WIDTH), lambda i, j: (i, j, 0)),
            pl.BlockSpec((1, N_KV_HEADS, gw, ts), lambda i, j: (i, 0, 0, j)),
            pl.BlockSpec((1, N_KV_HEADS, ts, gw), lambda i, j: (i, 0, j, 0)),
        ],
        out_shape=[
            jax.ShapeDtypeStruct((b, s, ATTN_WIDTH), BF16),
            jax.ShapeDtypeStruct((b, N_KV_HEADS, gw, s), BF16),
            jax.ShapeDtypeStruct((b, N_KV_HEADS, s, gw), BF16),
        ],
        compiler_params=_params(("parallel", "parallel")),
        name="qk_prep",
    )(qkv, qkv, qkv, cos_t, sin_t, qw, kw, seg_ones, rep_mat)


def _attn_kernel(q_ref, kt_ref, vd_ref, o_ref, s_scr, p_scr):
    tq = q_ref.shape[1]
    s_len = kt_ref.shape[3]
    kc = min(KEY_CHUNK, s_len)
    q = q_ref[0]
    head_of_lane = lax.broadcasted_iota(jnp.int32, q.shape, 1) // HEAD_DIM
    zero = jnp.zeros_like(q)
    qm = jnp.concatenate([jnp.where(head_of_lane == h, q, zero) for h in range(REP)], axis=0)
    m = None
    for c in range(s_len // kc):
        sc = _dot(qm, kt_ref[0, 0, :, c * kc:(c + 1) * kc])
        s_scr[:, c * kc:(c + 1) * kc] = sc
        mc = jnp.max(sc, axis=1, keepdims=True)
        m = mc if m is None else jnp.maximum(m, mc)
    denom = jnp.zeros((REP * tq, 1), F32)
    for c in range(s_len // kc):
        p = jnp.exp2(s_scr[:, c * kc:(c + 1) * kc] - m)
        denom = denom + jnp.sum(p, axis=1, keepdims=True)
        p_scr[:, c * kc:(c + 1) * kc] = p.astype(BF16)
    o = _dot(p_scr[...], vd_ref[0, 0]) / denom
    out = jnp.zeros((tq, REP * HEAD_DIM), F32)
    for h in range(REP):
        out = jnp.where(head_of_lane == h, o[h * tq:(h + 1) * tq], out)
    o_ref[0] = out.astype(o_ref.dtype)


def _attention(qn, kt_dup, v_dup):
    b, s, _ = qn.shape
    tq = min(TQ, s)
    gw = REP * HEAD_DIM
    grid = (b, N_KV_HEADS, s // tq)
    return pl.pallas_call(
        _attn_kernel,
        grid=grid,
        in_specs=[
            pl.BlockSpec((1, tq, gw), lambda i, g, j: (i, j, g)),
            pl.BlockSpec((1, 1, gw, s), lambda i, g, j: (i, g, 0, 0)),
            pl.BlockSpec((1, 1, s, gw), lambda i, g, j: (i, g, 0, 0)),
        ],
        out_specs=pl.BlockSpec((1, tq, gw), lambda i, g, j: (i, j, g)),
        out_shape=jax.ShapeDtypeStruct((b, s, ATTN_WIDTH), BF16),
        scratch_shapes=[
            pltpu.VMEM((REP * tq, s), F32),
            pltpu.VMEM((REP * tq, s), BF16),
        ],
        compiler_params=_params(("parallel", "parallel", "arbitrary")),
        name="attention",
    )(qn, kt_dup, v_dup)


def _split_hi_lo(v):
    hi = v.astype(BF16)
    lo = (v - hi.astype(F32)).astype(BF16)
    return hi, lo


def _softplus(v):
    return jnp.maximum(v, 0.0) + jnp.log1p(jnp.exp(-jnp.abs(v)))


def _silu(v):
    return v * (1.0 / (1.0 + jnp.exp(-v)))


def _conv_silu_chunk(src_ref, w_ref, b_ref, win_ref, c, n_chunks):
    s_len = src_ref.shape[1]
    r0 = pl.multiple_of(c * CHUNK, CHUNK)
    prev_start = pl.multiple_of(jnp.maximum(r0 - HALO, 0), HALO)
    next_start = pl.multiple_of(jnp.minimum(r0 + CHUNK, s_len - HALO), HALO)
    has_prev = (c > 0).astype(F32)
    has_next = (c < n_chunks - 1).astype(F32)
    win_ref[0:HALO, :] = src_ref[0, pl.ds(prev_start, HALO), :].astype(F32) * has_prev
    win_ref[HALO:HALO + CHUNK, :] = src_ref[0, pl.ds(r0, CHUNK), :].astype(F32)
    win_ref[HALO + CHUNK:2 * HALO + CHUNK, :] = src_ref[0, pl.ds(next_start, HALO), :].astype(F32) * has_next
    pad = D_CONV // 2
    acc = jnp.zeros((CHUNK, win_ref.shape[1]), F32) + b_ref[...]
    for k in range(D_CONV):
        off = HALO - pad + k
        acc = acc + win_ref[off:off + CHUNK, :] * w_ref[k:k + 1, :]
    return _silu(acc)


def _ssd_kernel(xs_ref, bm_ref, cm_ref, z_ref, dt_ref, dtt_ref,
                cwx_ref, cwb_ref, cwc_ref, cbx_ref, cbb_ref, cbc_ref,
                dtb_ref, alog_ref, dtbt_ref, alogt_ref, dskip_ref, normw_ref,
                expf_ref, expb_ref, tril_ref, triu_ref,
                o_ref,
                xc_scr, b_scr, bt_scr, c_scr, y_scr, st_scr, winx_scr, winb_scr):
    s_len = xs_ref.shape[1]
    n_chunks = s_len // CHUNK
    hp = HEADS_PER_GROUP

    def conv_body(c, carry):
        r0 = pl.multiple_of(c * CHUNK, CHUNK)
        xc = _conv_silu_chunk(xs_ref, cwx_ref, cbx_ref, winx_scr, c, n_chunks)
        xc_scr[pl.ds(r0, CHUNK), :] = xc.astype(BF16)
        bc = _conv_silu_chunk(bm_ref, cwb_ref, cbb_ref, winb_scr, c, n_chunks)
        b_scr[pl.ds(r0, CHUNK), :] = bc.astype(BF16)
        bt_scr[:, pl.ds(r0, CHUNK)] = bc.T.astype(BF16)
        cc = _conv_silu_chunk(cm_ref, cwc_ref, cbc_ref, winb_scr, c, n_chunks)
        c_scr[pl.ds(r0, CHUNK), :] = cc.astype(BF16)
        return carry

    lax.fori_loop(0, n_chunks, conv_body, 0)

    row = lax.broadcasted_iota(jnp.int32, (CHUNK, CHUNK), 0)
    col = lax.broadcasted_iota(jnp.int32, (CHUNK, CHUNK), 1)
    lane = lax.broadcasted_iota(jnp.int32, (CHUNK, LANES), 1)
    lane_lt_half = lane < SSM_HEAD_DIM
    neg_a = -jnp.exp(alog_ref[0])
    neg_a_t = -jnp.exp(alogt_ref[0])
    dskip = dskip_ref[...]

    def scan(direction):
        forward = direction == 0
        tri = tril_ref[...] if forward else triu_ref[...]
        tri_t = triu_ref[...] if forward else tril_ref[...]
        keep = (row >= col) if forward else (row <= col)
        expand = expf_ref[...] if forward else expb_ref[...]
        lane0 = 0 if forward else hp
        last = CHUNK - 1 if forward else 0

        def body(i, carry):
            c = i if forward else n_chunks - 1 - i
            r0 = pl.multiple_of(c * CHUNK, CHUNK)
            dts = _softplus(dt_ref[0, pl.ds(r0, CHUNK), :] + dtb_ref[0])
            a = dts * neg_a
            a_hi, a_lo = _split_hi_lo(a)
            acs2 = _dot(tri, jnp.concatenate([a_hi, a_lo], axis=1))
            acs = acs2[:, :LANES] + acs2[:, LANES:]
            dts_t = _softplus(dtt_ref[0, 0, :, pl.ds(r0, CHUNK)] + dtbt_ref[0])
            a_t = dts_t * neg_a_t
            at_hi, at_lo = _split_hi_lo(a_t)
            acst2 = _dot(jnp.concatenate([at_hi, at_lo], axis=0), tri_t)
            acs_t = acst2[:DT_ROWS] + acst2[DT_ROWS:]
            d_hi, d_lo = _split_hi_lo(dts)
            c_hi, c_lo = _split_hi_lo(acs)
            packed = jnp.concatenate([d_hi, d_lo, c_hi, c_lo], axis=1)
            ex = _dot(packed, expand)
            dt_exp = ex[:, :GROUP_WIDTH]
            acs_exp = ex[:, GROUP_WIDTH:]
            acs_last = acs_exp[last:last + 1, :]

            xc = xc_scr[pl.ds(r0, CHUNK), :].astype(F32)
            bc = b_scr[pl.ds(r0, CHUNK), :]
            bt = bt_scr[:, pl.ds(r0, CHUNK)]
            cc = c_scr[pl.ds(r0, CHUNK), :]
            xdt = xc * dt_exp
            xdt_b = xdt.astype(BF16)
            xd = (xdt * jnp.exp(acs_last - acs_exp)).astype(BF16)

            state = st_scr[...]
            y = _dot(cc, state.astype(BF16)) * jnp.exp(acs_exp)
            st_scr[...] = state * jnp.exp(acs_last) + _dot(bt, xd)

            cb = _dot_nt(cc, bc)
            pairs = []
            for j in range(hp // 2):
                ms = []
                for h in (2 * j, 2 * j + 1):
                    seg = acs[:, lane0 + h:lane0 + h + 1] - acs_t[lane0 + h:lane0 + h + 1, :]
                    lmat = jnp.exp(jnp.where(keep, seg, -jnp.inf))
                    ms.append((cb * lmat).astype(BF16))
                xp = xdt_b[:, j * LANES:(j + 1) * LANES]
                zero = jnp.zeros_like(xp)
                rhs = jnp.concatenate([jnp.where(lane_lt_half, xp, zero),
                                       jnp.where(lane_lt_half, zero, xp)], axis=0)
                pairs.append(_dot(jnp.concatenate(ms, axis=1), rhs))
            y = y + jnp.concatenate(pairs, axis=1)
            if forward:
                y_scr[pl.ds(r0, CHUNK), :] = y + dskip * xc
            else:
                y_scr[pl.ds(r0, CHUNK), :] += y
            return carry

        st_scr[...] = jnp.zeros_like(st_scr)
        lax.fori_loop(0, n_chunks, body, 0)

    scan(0)
    scan(1)

    rows = min(EPI_ROWS, s_len)

    def epi_body(i, carry):
        r0 = pl.multiple_of(i * rows, rows)
        y = y_scr[pl.ds(r0, rows), :] * _silu(z_ref[0, pl.ds(r0, rows), :].astype(F32))
        ms = jnp.mean(y * y, axis=-1, keepdims=True)
        o_ref[0, pl.ds(r0, rows), :] = (y * lax.rsqrt(ms + NORM_EPS) * normw_ref[...]).astype(o_ref.dtype)
        return carry

    lax.fori_loop(0, s_len // rows, epi_body, 0)


def _ssd(xbc, z, dt_pad, dt_t, cw, cb, dtb, alog, dtb_t, alog_t, dskip, normw, exp_f, exp_b, tril, triu):
    b, s, _ = xbc.shape
    grid = (b, N_SSM_GROUPS)
    gw = GROUP_WIDTH
    x_blocks = D_INNER // gw
    b_block0 = D_INNER // D_STATE
    c_block0 = b_block0 + N_SSM_GROUPS
    seq = lambda width, off: pl.BlockSpec((1, s, width), lambda i, g: (i, 0, off + g))
    wspec = lambda rows, width, off: pl.BlockSpec((rows, width), lambda i, g: (0, off + g))
    gspec = lambda rows: pl.BlockSpec((1, rows, LANES), lambda i, g: (g, 0, 0))
    del x_blocks
    return pl.pallas_call(
        _ssd_kernel,
        grid=grid,
        in_specs=[
            seq(gw, 0), seq(D_STATE, b_block0), seq(D_STATE, c_block0),
            seq(gw, 0),
            seq(LANES, 0),
            pl.BlockSpec((1, 1, DT_ROWS, s), lambda i, g: (i, g, 0, 0)),
            wspec(D_CONV, gw, 0), wspec(D_CONV, D_STATE, b_block0), wspec(D_CONV, D_STATE, c_block0),
            wspec(1, gw, 0), wspec(1, D_STATE, b_block0), wspec(1, D_STATE, c_block0),
            gspec(1), gspec(1), gspec(DT_ROWS), gspec(DT_ROWS),
            wspec(1, gw, 0), wspec(1, gw, 0),
            _const_spec((4 * LANES, 2 * gw)), _const_spec((4 * LANES, 2 * gw)),
            _const_spec((CHUNK, CHUNK)), _const_spec((CHUNK, CHUNK)),
        ],
        out_specs=pl.BlockSpec((1, s, gw), lambda i, g: (i, 0, g)),
        out_shape=jax.ShapeDtypeStruct((b, s, D_INNER), BF16),
        scratch_shapes=[
            pltpu.VMEM((s, gw), BF16),
            pltpu.VMEM((s, D_STATE), BF16),
            pltpu.VMEM((D_STATE, s), BF16),
            pltpu.VMEM((s, D_STATE), BF16),
            pltpu.VMEM((s, gw), F32),
            pltpu.VMEM((D_STATE, gw), F32),
            pltpu.VMEM((CHUNK + 2 * HALO, gw), F32),
            pltpu.VMEM((CHUNK + 2 * HALO, D_STATE), F32),
        ],
        compiler_params=_params(("parallel", "arbitrary")),
        name="ssd",
    )(xbc, xbc, xbc, z, dt_pad, dt_t, cw, cw, cw, cb, cb, cb,
      dtb, alog, dtb_t, alog_t, dskip, normw, exp_f, exp_b, tril, triu)


def _outmlp_kernel(attn_ref, ssm_ref, x_ref, woa_ref, wos_ref, ln2_ref, wup_ref, wdn_ref, fin_ref,
                   o_ref, *, final_norm):
    x1 = x_ref[0] + _dot(attn_ref[0], woa_ref[...]) + _dot(ssm_ref[0], wos_ref[...])
    ms = jnp.mean(x1 * x1, axis=-1, keepdims=True)
    h = (x1 * lax.rsqrt(ms + NORM_EPS) * ln2_ref[...]).astype(BF16)
    u = jnp.maximum(_dot(h, wup_ref[...]), 0.0)
    x2 = x1 + _dot((u * u).astype(BF16), wdn_ref[...])
    if final_norm:
        ms2 = jnp.mean(x2 * x2, axis=-1, keepdims=True)
        x2 = x2 * lax.rsqrt(ms2 + NORM_EPS) * fin_ref[...]
    o_ref[0] = x2


def _out_mlp(attn, ssm, x, wo_attn, wo_ssm, ln2, w_up, w_down, fin_w, final_norm):
    b, s, _ = x.shape
    tm = min(TM_MLP, s)
    grid = (b, s // tm)
    tok = lambda width: pl.BlockSpec((1, tm, width), lambda i, j: (i, j, 0))
    return pl.pallas_call(
        functools.partial(_outmlp_kernel, final_norm=final_norm),
        grid=grid,
        in_specs=[
            tok(ATTN_WIDTH), tok(D_INNER), tok(D_MODEL),
            _const_spec((ATTN_WIDTH, D_MODEL)), _const_spec((D_INNER, D_MODEL)),
            _const_spec((1, D_MODEL)),
            _const_spec((D_MODEL, D_FF)), _const_spec((D_FF, D_MODEL)),
            _const_spec((1, D_MODEL)),
        ],
        out_specs=tok(D_MODEL),
        out_shape=jax.ShapeDtypeStruct((b, s, D_MODEL), F32),
        compiler_params=_params(("parallel", "parallel")),
        name="out_mlp",
    )(attn, ssm, x, wo_attn, wo_ssm, ln2, w_up, w_down, fin_w)


def _rope_tables(seq):
    rows = seq // GRID_W
    row_ids = jnp.repeat(jnp.arange(rows, dtype=jnp.int32), GRID_W)
    col_ids = jnp.tile(jnp.arange(GRID_W, dtype=jnp.int32), rows)
    half = HEAD_DIM // 2
    inv_freq = ROPE_THETA ** (-jnp.arange(0, half, 2, dtype=F32) / half)

    def ang(pos):
        a = pos.astype(F32)[:, None] * inv_freq[None, :]
        return jnp.concatenate([a, a], axis=-1)

    a = jnp.concatenate([ang(row_ids), ang(col_ids)], axis=-1)
    sign = jnp.tile(jnp.concatenate([-jnp.ones((half // 2,), F32), jnp.ones((half // 2,), F32)]), 2)
    cos = jnp.cos(a)
    sin = jnp.sin(a) * sign[None, :]
    return jnp.tile(cos, (1, LANES // HEAD_DIM)), jnp.tile(sin, (1, LANES // HEAD_DIM))


def _constants():
    seg = np.kron(np.eye(N_Q_HEADS, dtype=np.float32), np.ones((HEAD_DIM, HEAD_DIM), np.float32))
    gw = REP * HEAD_DIM
    rep = np.zeros((KV_WIDTH, N_KV_HEADS * gw), np.float32)
    for g in range(N_KV_HEADS):
        for r in range(REP):
            for d in range(HEAD_DIM):
                rep[g * HEAD_DIM + d, g * gw + r * HEAD_DIM + d] = 1.0
    def expansion(lane0):
        e = np.zeros((4 * LANES, 2 * GROUP_WIDTH), np.float32)
        for h in range(HEADS_PER_GROUP):
            cols = slice(h * SSM_HEAD_DIM, (h + 1) * SSM_HEAD_DIM)
            e[0 * LANES + lane0 + h, cols] = 1.0
            e[1 * LANES + lane0 + h, cols] = 1.0
            cols2 = slice(GROUP_WIDTH + h * SSM_HEAD_DIM, GROUP_WIDTH + (h + 1) * SSM_HEAD_DIM)
            e[2 * LANES + lane0 + h, cols2] = 1.0
            e[3 * LANES + lane0 + h, cols2] = 1.0
        return e
    tril = np.tril(np.ones((CHUNK, CHUNK), np.float32))
    return dict(
        seg=jnp.asarray(seg, BF16), rep=jnp.asarray(rep, BF16),
        exp_f=jnp.asarray(expansion(0), BF16), exp_b=jnp.asarray(expansion(HEADS_PER_GROUP), BF16),
        tril=jnp.asarray(tril, BF16), triu=jnp.asarray(tril.T, BF16),
    )


def _group_rows(vf, vb):
    return jnp.concatenate([vf.reshape(N_SSM_GROUPS, HEADS_PER_GROUP),
                            vb.reshape(N_SSM_GROUPS, HEADS_PER_GROUP)], axis=1)


def _layer_weights(w_in, dt_bias_f, dt_bias_b, a_log_f, a_log_b):
    wdt = w_in[:, XBC_OFF + CONV_DIM:]
    wf = wdt[:, :N_SSM_HEADS].reshape(D_MODEL, N_SSM_GROUPS, HEADS_PER_GROUP)
    wb = wdt[:, N_SSM_HEADS:].reshape(D_MODEL, N_SSM_GROUPS, HEADS_PER_GROUP)
    grp = jnp.concatenate([wf, wb], axis=-1)
    wdt_pad = jnp.pad(grp, ((0, 0), (0, 0), (0, LANES - DT_ROWS))).reshape(D_MODEL, DT_PAD)
    w_all = jnp.concatenate([w_in[:, :DT_OFF], wdt_pad], axis=1).astype(BF16)
    wdt_t = grp.reshape(D_MODEL, N_SSM_GROUPS * DT_ROWS).T.astype(BF16)
    pad_lanes = lambda v: jnp.pad(v, ((0, 0), (0, LANES - DT_ROWS)))[:, None, :]
    dtb = _group_rows(dt_bias_f, dt_bias_b)
    alog = _group_rows(a_log_f, a_log_b)
    bcast = lambda v: jnp.broadcast_to(v[:, :, None], (N_SSM_GROUPS, DT_ROWS, LANES))
    return w_all, wdt_t, pad_lanes(dtb), pad_lanes(alog), bcast(dtb), bcast(alog)


def kernel(x, ln1_w, w_in, conv_w, conv_b, dt_bias_fwd, dt_bias_bwd, a_log_fwd, a_log_bwd, d_skip,
           ssm_norm_w, q_norm_w, k_norm_w, w_out, ln2_w, w_up, w_down, final_norm_w):
    b, s, _ = x.shape
    depth = w_in.shape[0]
    consts = _constants()
    cos_t, sin_t = _rope_tables(s)
    row = lambda v: v.reshape(1, -1).astype(F32)
    for i in range(depth):
        w_all, wdt_t, dtb, alog, dtb_t, alog_t = _layer_weights(
            w_in[i], dt_bias_fwd[i], dt_bias_bwd[i], a_log_fwd[i], a_log_bwd[i])
        qkv, z, xbc, dt_pad, dt_t = _in_proj(x, row(ln1_w[i]), w_all, wdt_t)
        qn, kt_dup, v_dup = _qk_prep(
            qkv, cos_t, sin_t, row(jnp.tile(q_norm_w[i], N_Q_HEADS)), row(jnp.tile(k_norm_w[i], N_KV_HEADS)),
            consts["seg"], consts["rep"])
        attn = _attention(qn, kt_dup, v_dup)
        ssm = _ssd(
            xbc, z, dt_pad, dt_t.reshape(b, N_SSM_GROUPS, DT_ROWS, s),
            conv_w[i].astype(F32), row(conv_b[i]), dtb, alog, dtb_t, alog_t,
            row(jnp.repeat(d_skip[i], SSM_HEAD_DIM)), row(ssm_norm_w[i]),
            consts["exp_f"], consts["exp_b"], consts["tril"], consts["triu"])
        x = _out_mlp(
            attn, ssm, x, w_out[i, :ATTN_WIDTH].astype(BF16), w_out[i, ATTN_WIDTH:].astype(BF16),
            row(ln2_w[i]), w_up[i].astype(BF16), w_down[i].astype(BF16), row(final_norm_w),
            final_norm=(i == depth - 1))
    return x
```

```python
import functools

import numpy as np
import jax
import jax.numpy as jnp
from jax import lax
from jax.experimental import pallas as pl
from jax.experimental.pallas import tpu as pltpu

F32 = jnp.float32
BF16 = jnp.bfloat16

D_MODEL = 1024
GRID_W = 64
N_Q_HEADS = 16
N_KV_HEADS = 4
HEAD_DIM = 64
REP = N_Q_HEADS // N_KV_HEADS
ATTN_WIDTH = N_Q_HEADS * HEAD_DIM
KV_WIDTH = N_KV_HEADS * HEAD_DIM
ROPE_THETA = 10000.0
D_INNER = 2048
SSM_HEAD_DIM = 64
N_SSM_HEADS = D_INNER // SSM_HEAD_DIM
N_SSM_GROUPS = 4
HEADS_PER_GROUP = N_SSM_HEADS // N_SSM_GROUPS
GROUP_WIDTH = D_INNER // N_SSM_GROUPS
D_STATE = 128
D_CONV = 5
CHUNK = 128
CONV_DIM = D_INNER + 2 * N_SSM_GROUPS * D_STATE
D_FF = 4 * D_MODEL
NORM_EPS = 1e-5
QK_EPS = 1e-6
LOG2E = 1.4426950408889634

LANES = 128
VMEM_LIMIT_BYTES = 58 * 1024 * 1024

QKV_WIDTH = ATTN_WIDTH + 2 * KV_WIDTH
Z_OFF = QKV_WIDTH
XBC_OFF = Z_OFF + D_INNER
DT_OFF = XBC_OFF + CONV_DIM
DT_PAD = N_SSM_GROUPS * LANES
W_ALL_WIDTH = DT_OFF + DT_PAD
DT_ROWS = 2 * HEADS_PER_GROUP

TM_PROJ = 512
TS_PREP = 512
TQ = 256
KEY_CHUNK = 512
VT_ROWS = HEAD_DIM + 16
TM_MLP = 256
EPI_ROWS = 256
HALO = 16


def _dot(a, b):
    return jnp.dot(a, b, preferred_element_type=F32)


def _dot_nt(a, b):
    return lax.dot_general(a, b, (((1,), (1,)), ((), ())), preferred_element_type=F32)


def _params(semantics):
    return pltpu.CompilerParams(dimension_semantics=semantics, vmem_limit_bytes=VMEM_LIMIT_BYTES)


def _const_spec(shape):
    nd = len(shape)
    return pl.BlockSpec(shape, lambda *_: (0,) * nd, pipeline_mode=pl.Buffered(1))


def _inproj_kernel(x_ref, ln_ref, w_ref, wdt_t_ref, qkv_ref, z_ref, xbc_ref, dt_ref, dtt_ref):
    x = x_ref[0]
    ms = jnp.mean(x * x, axis=-1, keepdims=True)
    hn = (x * lax.rsqrt(ms + NORM_EPS) * ln_ref[...]).astype(BF16)
    qkv_ref[0] = _dot(hn, w_ref[:, 0:QKV_WIDTH]).astype(BF16)
    z_ref[0] = _dot(hn, w_ref[:, Z_OFF:XBC_OFF]).astype(BF16)
    xbc_ref[0] = _dot(hn, w_ref[:, XBC_OFF:DT_OFF]).astype(BF16)
    dt_ref[0] = _dot(hn, w_ref[:, DT_OFF:W_ALL_WIDTH])
    dtt_ref[0] = _dot_nt(wdt_t_ref[...], hn)


def _in_proj(x, ln_w, w_all, wdt_t):
    b, s, _ = x.shape
    tm = min(TM_PROJ, s)
    grid = (b, s // tm)
    tok = lambda width: pl.BlockSpec((1, tm, width), lambda i, j: (i, j, 0))
    return pl.pallas_call(
        _inproj_kernel,
        grid=grid,
        in_specs=[
            tok(D_MODEL),
            _const_spec((1, D_MODEL)),
            _const_spec((D_MODEL, W_ALL_WIDTH)),
            _const_spec((N_SSM_GROUPS * DT_ROWS, D_MODEL)),
        ],
        out_specs=[
            tok(QKV_WIDTH), tok(D_INNER), tok(CONV_DIM), tok(DT_PAD),
            pl.BlockSpec((1, N_SSM_GROUPS * DT_ROWS, tm), lambda i, j: (i, 0, j)),
        ],
        out_shape=[
            jax.ShapeDtypeStruct((b, s, QKV_WIDTH), BF16),
            jax.ShapeDtypeStruct((b, s, D_INNER), BF16),
            jax.ShapeDtypeStruct((b, s, CONV_DIM), BF16),
            jax.ShapeDtypeStruct((b, s, DT_PAD), F32),
            jax.ShapeDtypeStruct((b, N_SSM_GROUPS * DT_ROWS, s), F32),
        ],
        compiler_params=_params(("parallel", "parallel")),
        name="in_proj",
    )(x, ln_w, w_all, wdt_t)


def _norm_rope(t, w, seg_ones, cos, sin_signed, post_scale):
    width = t.shape[1]
    ss = _dot((t * t).astype(BF16), seg_ones) * (1.0 / HEAD_DIM)
    y = t * lax.rsqrt(ss + QK_EPS) * w
    reps = width // LANES
    cosw = jnp.concatenate([cos] * reps, axis=1)
    sinw = jnp.concatenate([sin_signed] * reps, axis=1)
    quarter = HEAD_DIM // 4
    upper = pltpu.roll(y, width - quarter, 1)
    lower = pltpu.roll(y, quarter, 1)
    lane = lax.broadcasted_iota(jnp.int32, y.shape, 1)
    first_half = (lane % (2 * quarter)) < quarter
    rot = jnp.where(first_half, upper, lower)
    return (y * cosw + rot * sinw) * post_scale


def _qkprep_kernel(q_ref, k_ref, v_ref, cos_ref, sin_ref, qw_ref, kw_ref, seg_ref,
                   qt_ref, kn_ref, vt_ref):
    cos = cos_ref[...]
    sin = sin_ref[...]
    q = _norm_rope(q_ref[0].astype(F32), qw_ref[...], seg_ref[...], cos, sin,
                   (HEAD_DIM ** -0.5) * LOG2E)
    qt_ref[0] = q.T.astype(BF16)
    k = _norm_rope(k_ref[0].astype(F32), kw_ref[...], seg_ref[0:KV_WIDTH, 0:KV_WIDTH], cos, sin, 1.0)
    kn_ref[0] = k.astype(BF16)
    vt = v_ref[0].astype(F32).T.astype(BF16)
    ones = jnp.ones((VT_ROWS - HEAD_DIM, vt.shape[1]), BF16)
    vt_ref[0] = jnp.concatenate(
        [piece for g in range(N_KV_HEADS) for piece in (vt[g * HEAD_DIM:(g + 1) * HEAD_DIM], ones)], axis=0)


def _qk_prep(qkv, cos_t, sin_t, qw, kw, seg_ones):
    b, s, _ = qkv.shape
    ts = min(TS_PREP, s)
    grid = (b, s // ts)
    return pl.pallas_call(
        _qkprep_kernel,
        grid=grid,
        in_specs=[
            pl.BlockSpec((1, ts, ATTN_WIDTH), lambda i, j: (i, j, 0)),
            pl.BlockSpec((1, ts, KV_WIDTH), lambda i, j: (i, j, ATTN_WIDTH // KV_WIDTH)),
            pl.BlockSpec((1, ts, KV_WIDTH), lambda i, j: (i, j, ATTN_WIDTH // KV_WIDTH + 1)),
            pl.BlockSpec((ts, LANES), lambda i, j: (j, 0)),
            pl.BlockSpec((ts, LANES), lambda i, j: (j, 0)),
            _const_spec((1, ATTN_WIDTH)),
            _const_spec((1, KV_WIDTH)),
            _const_spec((ATTN_WIDTH, ATTN_WIDTH)),
        ],
        out_specs=[
            pl.BlockSpec((1, ATTN_WIDTH, ts), lambda i, j: (i, 0, j)),
            pl.BlockSpec((1, ts, KV_WIDTH), lambda i, j: (i, j, 0)),
            pl.BlockSpec((1, N_KV_HEADS * VT_ROWS, ts), lambda i, j: (i, 0, j)),
        ],
        out_shape=[
            jax.ShapeDtypeStruct((b, ATTN_WIDTH, s), BF16),
            jax.ShapeDtypeStruct((b, s, KV_WIDTH), BF16),
            jax.ShapeDtypeStruct((b, N_KV_HEADS * VT_ROWS, s), BF16),
        ],
        compiler_params=_params(("parallel", "parallel")),
        name="qk_prep",
    )(qkv, qkv, qkv, cos_t, sin_t, qw, kw, seg_ones)


def _attn_kernel(qt_ref, k_ref, vt_ref, o_ref, qe_scr):
    g = pl.program_id(1)
    tq = qt_ref.shape[2]
    s_len = k_ref.shape[1]
    kc = min(KEY_CHUNK, s_len)
    qt = qt_ref[0]
    q_cols = jnp.concatenate([qt[r * HEAD_DIM:(r + 1) * HEAD_DIM, :] for r in range(REP)], axis=1)
    qe_scr[...] = jnp.zeros_like(qe_scr)
    qe_scr[pl.ds(pl.multiple_of(g * HEAD_DIM, HEAD_DIM), HEAD_DIM), :] = q_cols
    qe = qe_scr[...]
    m = jnp.full((1, REP * tq), -jnp.inf, F32)
    acc = jnp.zeros((VT_ROWS, REP * tq), F32)
    n_kc = s_len // kc
    scores = lambda c: _dot(k_ref[0, c * kc:(c + 1) * kc, :], qe)
    st_next = scores(0)
    for c in range(n_kc):
        st = st_next
        if c + 1 < n_kc:
            st_next = scores(c + 1)
        m_new = jnp.maximum(m, jnp.max(st, axis=0, keepdims=True))
        alpha = jnp.exp2(m - m_new)
        p = jnp.exp2(st - m_new)
        acc = alpha * acc + _dot(vt_ref[0, :, c * kc:(c + 1) * kc], p.astype(BF16))
        m = m_new
    ot = (acc[:HEAD_DIM] / acc[HEAD_DIM:HEAD_DIM + 1]).T
    out = jnp.concatenate([ot[r * tq:(r + 1) * tq, :] for r in range(REP)], axis=1)
    o_ref[0] = out.astype(o_ref.dtype)


def _attention(qt, kn, vt):
    b, _, s = qt.shape
    tq = min(TQ, s)
    gw = REP * HEAD_DIM
    grid = (b, N_KV_HEADS, s // tq)
    return pl.pallas_call(
        _attn_kernel,
        grid=grid,
        in_specs=[
            pl.BlockSpec((1, gw, tq), lambda i, g, j: (i, g, j)),
            pl.BlockSpec((1, s, KV_WIDTH), lambda i, g, j: (i, 0, 0)),
            pl.BlockSpec((1, VT_ROWS, s), lambda i, g, j: (i, g, 0)),
        ],
        out_specs=pl.BlockSpec((1, tq, gw), lambda i, g, j: (i, j, g)),
        out_shape=jax.ShapeDtypeStruct((b, s, ATTN_WIDTH), BF16),
        scratch_shapes=[pltpu.VMEM((KV_WIDTH, REP * tq), BF16)],
        compiler_params=_params(("parallel", "parallel", "arbitrary")),
        name="attention",
    )(qt, kn, vt)


def _split_hi_lo(v):
    hi = v.astype(BF16)
    lo = (v - hi.astype(F32)).astype(BF16)
    return hi, lo


def _softplus(v):
    return jnp.maximum(v, 0.0) + jnp.log1p(jnp.exp(-jnp.abs(v)))


def _silu(v):
    return v * (1.0 / (1.0 + jnp.exp(-v)))


def _conv_silu_chunk(src_ref, w_ref, b_ref, win_ref, c, n_chunks):
    s_len = src_ref.shape[1]
    r0 = pl.multiple_of(c * CHUNK, CHUNK)
    prev_start = pl.multiple_of(jnp.maximum(r0 - HALO, 0), HALO)
    next_start = pl.multiple_of(jnp.minimum(r0 + CHUNK, s_len - HALO), HALO)
    has_prev = (c > 0).astype(F32)
    has_next = (c < n_chunks - 1).astype(F32)
    win_ref[0:HALO, :] = src_ref[0, pl.ds(prev_start, HALO), :].astype(F32) * has_prev
    win_ref[HALO:HALO + CHUNK, :] = src_ref[0, pl.ds(r0, CHUNK), :].astype(F32)
    win_ref[HALO + CHUNK:2 * HALO + CHUNK, :] = src_ref[0, pl.ds(next_start, HALO), :].astype(F32) * has_next
    pad = D_CONV // 2
    acc = jnp.zeros((CHUNK, win_ref.shape[1]), F32) + b_ref[...]
    for k in range(D_CONV):
        off = HALO - pad + k
        acc = acc + win_ref[off:off + CHUNK, :] * w_ref[k:k + 1, :]
    return _silu(acc)


def _ssd_kernel(xs_ref, bm_ref, cm_ref, z_ref, dt_ref, dtt_ref,
                cwx_ref, cwb_ref, cwc_ref, cbx_ref, cbb_ref, cbc_ref,
                dtb_ref, alog_ref, dtbt_ref, alogt_ref, dskip_ref, normw_ref,
                expf_ref, expb_ref, tril_ref, triu_ref,
                o_ref,
                xc_scr, b_scr, bt_scr, c_scr, y_scr, stf_scr, stb_scr, winx_scr, winb_scr):
    s_len = xs_ref.shape[1]
    n_chunks = s_len // CHUNK
    hp = HEADS_PER_GROUP
    dskip = dskip_ref[...]

    def conv_body(c, carry):
        r0 = pl.multiple_of(c * CHUNK, CHUNK)
        xc = _conv_silu_chunk(xs_ref, cwx_ref, cbx_ref, winx_scr, c, n_chunks)
        xc_scr[pl.ds(r0, CHUNK), :] = xc.astype(BF16)
        y_scr[pl.ds(r0, CHUNK), :] = dskip * xc
        bc = _conv_silu_chunk(bm_ref, cwb_ref, cbb_ref, winb_scr, c, n_chunks)
        b_scr[pl.ds(r0, CHUNK), :] = bc.astype(BF16)
        bt_scr[:, pl.ds(r0, CHUNK)] = bc.T.astype(BF16)
        cc = _conv_silu_chunk(cm_ref, cwc_ref, cbc_ref, winb_scr, c, n_chunks)
        c_scr[pl.ds(r0, CHUNK), :] = cc.astype(BF16)
        return carry

    lax.fori_loop(0, n_chunks, conv_body, 0)

    row = lax.broadcasted_iota(jnp.int32, (CHUNK, CHUNK), 0)
    col = lax.broadcasted_iota(jnp.int32, (CHUNK, CHUNK), 1)
    lane = lax.broadcasted_iota(jnp.int32, (CHUNK, LANES), 1)
    lane_lt_half = lane < SSM_HEAD_DIM
    neg_a = -jnp.exp(alog_ref[0]) * LOG2E
    neg_a_t = -jnp.exp(alogt_ref[0]) * LOG2E

    def chunk_step(forward, c, st_scr):
        tri = tril_ref[...] if forward else triu_ref[...]
        tri_t = triu_ref[...] if forward else tril_ref[...]
        keep = (row >= col) if forward else (row <= col)
        expand = expf_ref[...] if forward else expb_ref[...]
        lane0 = 0 if forward else hp
        last = CHUNK - 1 if forward else 0
        r0 = pl.multiple_of(c * CHUNK, CHUNK)
        bc = b_scr[pl.ds(r0, CHUNK), :]
        bt = bt_scr[:, pl.ds(r0, CHUNK)]
        cc = c_scr[pl.ds(r0, CHUNK), :]
        dts = _softplus(dt_ref[0, pl.ds(r0, CHUNK), :] + dtb_ref[0])
        a_hi, a_lo = _split_hi_lo(dts * neg_a)
        acs2 = _dot(tri, jnp.concatenate([a_hi, a_lo], axis=1))
        dts_t = _softplus(dtt_ref[0, 0, :, pl.ds(r0, CHUNK)] + dtbt_ref[0])
        at_hi, at_lo = _split_hi_lo(dts_t * neg_a_t)
        acst2 = _dot(jnp.concatenate([at_hi, at_lo], axis=0), tri_t)
        cb = _dot_nt(cc, bc)
        yield
        acs = acs2[:, :LANES] + acs2[:, LANES:]
        acs_t = acst2[:DT_ROWS] + acst2[DT_ROWS:]
        d_hi, d_lo = _split_hi_lo(dts)
        c_hi, c_lo = _split_hi_lo(acs)
        packed = jnp.concatenate([d_hi, d_lo, c_hi, c_lo], axis=1)
        ex = _dot(packed, expand)
        ms = []
        for j in range(hp // 2):
            pair = []
            for h in (2 * j, 2 * j + 1):
                seg = acs[:, lane0 + h:lane0 + h + 1] - acs_t[lane0 + h:lane0 + h + 1, :]
                lmat = jnp.exp2(jnp.where(keep, seg, -jnp.inf))
                pair.append((cb * lmat).astype(BF16))
            ms.append(jnp.concatenate(pair, axis=1))
        yield
        dt_exp = ex[:, :GROUP_WIDTH]
        acs_exp = ex[:, GROUP_WIDTH:]
        acs_last = acs_exp[last:last + 1, :]
        xdt = xc_scr[pl.ds(r0, CHUNK), :].astype(F32) * dt_exp
        xdt_b = xdt.astype(BF16)
        xd = (xdt * jnp.exp2(acs_last - acs_exp)).astype(BF16)
        state = st_scr[...]
        y_off = _dot(cc, state.astype(BF16))
        st_new = _dot(bt, xd)
        pairs = []
        for j in range(hp // 2):
            xp = xdt_b[:, j * LANES:(j + 1) * LANES]
            zero = jnp.zeros_like(xp)
            rhs = jnp.concatenate([jnp.where(lane_lt_half, xp, zero),
                                   jnp.where(lane_lt_half, zero, xp)], axis=0)
            pairs.append(_dot(ms[j], rhs))
        yield
        st_scr[...] = state * jnp.exp2(acs_last) + st_new
        y_scr[pl.ds(r0, CHUNK), :] += y_off * jnp.exp2(acs_exp) + jnp.concatenate(pairs, axis=1)

    def scan_body(i, carry):
        active = [chunk_step(True, i, stf_scr), chunk_step(False, n_chunks - 1 - i, stb_scr)]
        while active:
            active = [gen for gen in active if next(gen, True) is None]
        return carry

    stf_scr[...] = jnp.zeros_like(stf_scr)
    stb_scr[...] = jnp.zeros_like(stb_scr)
    lax.fori_loop(0, n_chunks, scan_body, 0)

    rows = min(EPI_ROWS, s_len)

    def epi_body(i, carry):
        r0 = pl.multiple_of(i * rows, rows)
        y = y_scr[pl.ds(r0, rows), :] * _silu(z_ref[0, pl.ds(r0, rows), :].astype(F32))
        ms = jnp.mean(y * y, axis=-1, keepdims=True)
        o_ref[0, pl.ds(r0, rows), :] = (y * lax.rsqrt(ms + NORM_EPS) * normw_ref[...]).astype(o_ref.dtype)
        return carry

    lax.fori_loop(0, s_len // rows, epi_body, 0)


def _ssd(xbc, z, dt_pad, dt_t, cw, cb, dtb, alog, dtb_t, alog_t, dskip, normw, exp_f, exp_b, tril, triu):
    b, s, _ = xbc.shape
    grid = (b, N_SSM_GROUPS)
    gw = GROUP_WIDTH
    x_blocks = D_INNER // gw
    b_block0 = D_INNER // D_STATE
    c_block0 = b_block0 + N_SSM_GROUPS
    seq = lambda width, off: pl.BlockSpec((1, s, width), lambda i, g: (i, 0, off + g))
    wspec = lambda rows, width, off: pl.BlockSpec((rows, width), lambda i, g: (0, off + g))
    gspec = lambda rows: pl.BlockSpec((1, rows, LANES), lambda i, g: (g, 0, 0))
    del x_blocks
    return pl.pallas_call(
        _ssd_kernel,
        grid=grid,
        in_specs=[
            seq(gw, 0), seq(D_STATE, b_block0), seq(D_STATE, c_block0),
            seq(gw, 0),
            seq(LANES, 0),
            pl.BlockSpec((1, 1, DT_ROWS, s), lambda i, g: (i, g, 0, 0)),
            wspec(D_CONV, gw, 0), wspec(D_CONV, D_STATE, b_block0), wspec(D_CONV, D_STATE, c_block0),
            wspec(1, gw, 0), wspec(1, D_STATE, b_block0), wspec(1, D_STATE, c_block0),
            gspec(1), gspec(1), gspec(DT_ROWS), gspec(DT_ROWS),
            wspec(1, gw, 0), wspec(1, gw, 0),
            _const_spec((4 * LANES, 2 * gw)), _const_spec((4 * LANES, 2 * gw)),
            _const_spec((CHUNK, CHUNK)), _const_spec((CHUNK, CHUNK)),
        ],
        out_specs=pl.BlockSpec((1, s, gw), lambda i, g: (i, 0, g)),
        out_shape=jax.ShapeDtypeStruct((b, s, D_INNER), BF16),
        scratch_shapes=[
            pltpu.VMEM((s, gw), BF16),
            pltpu.VMEM((s, D_STATE), BF16),
            pltpu.VMEM((D_STATE, s), BF16),
            pltpu.VMEM((s, D_STATE), BF16),
            pltpu.VMEM((s, gw), F32),
            pltpu.VMEM((D_STATE, gw), F32),
            pltpu.VMEM((D_STATE, gw), F32),
            pltpu.VMEM((CHUNK + 2 * HALO, gw), F32),
            pltpu.VMEM((CHUNK + 2 * HALO, D_STATE), F32),
        ],
        compiler_params=_params(("parallel", "arbitrary")),
        name="ssd",
    )(xbc, xbc, xbc, z, dt_pad, dt_t, cw, cw, cw, cb, cb, cb,
      dtb, alog, dtb_t, alog_t, dskip, normw, exp_f, exp_b, tril, triu)


def _outmlp_kernel(attn_ref, ssm_ref, x_ref, woa_ref, wos_ref, ln2_ref, wup_ref, wdn_ref, fin_ref,
                   o_ref, *, final_norm):
    x1 = x_ref[0] + _dot(attn_ref[0], woa_ref[...]) + _dot(ssm_ref[0], wos_ref[...])
    ms = jnp.mean(x1 * x1, axis=-1, keepdims=True)
    h = (x1 * lax.rsqrt(ms + NORM_EPS) * ln2_ref[...]).astype(BF16)
    u = jnp.maximum(_dot(h, wup_ref[...]), 0.0)
    x2 = x1 + _dot((u * u).astype(BF16), wdn_ref[...])
    if final_norm:
        ms2 = jnp.mean(x2 * x2, axis=-1, keepdims=True)
        x2 = x2 * lax.rsqrt(ms2 + NORM_EPS) * fin_ref[...]
    o_ref[0] = x2


def _out_mlp(attn, ssm, x, wo_attn, wo_ssm, ln2, w_up, w_down, fin_w, final_norm):
    b, s, _ = x.shape
    tm = min(TM_MLP, s)
    grid = (b, s // tm)
    tok = lambda width: pl.BlockSpec((1, tm, width), lambda i, j: (i, j, 0))
    return pl.pallas_call(
        functools.partial(_outmlp_kernel, final_norm=final_norm),
        grid=grid,
        in_specs=[
            tok(ATTN_WIDTH), tok(D_INNER), tok(D_MODEL),
            _const_spec((ATTN_WIDTH, D_MODEL)), _const_spec((D_INNER, D_MODEL)),
            _const_spec((1, D_MODEL)),
            _const_spec((D_MODEL, D_FF)), _const_spec((D_FF, D_MODEL)),
            _const_spec((1, D_MODEL)),
        ],
        out_specs=tok(D_MODEL),
        out_shape=jax.ShapeDtypeStruct((b, s, D_MODEL), F32),
        compiler_params=_params(("parallel", "parallel")),
        name="out_mlp",
    )(attn, ssm, x, wo_attn, wo_ssm, ln2, w_up, w_down, fin_w)


def _rope_tables(seq):
    rows = seq // GRID_W
    row_ids = jnp.repeat(jnp.arange(rows, dtype=jnp.int32), GRID_W)
    col_ids = jnp.tile(jnp.arange(GRID_W, dtype=jnp.int32), rows)
    half = HEAD_DIM // 2
    inv_freq = ROPE_THETA ** (-jnp.arange(0, half, 2, dtype=F32) / half)

    def ang(pos):
        a = pos.astype(F32)[:, None] * inv_freq[None, :]
        return jnp.concatenate([a, a], axis=-1)

    a = jnp.concatenate([ang(row_ids), ang(col_ids)], axis=-1)
    sign = jnp.tile(jnp.concatenate([-jnp.ones((half // 2,), F32), jnp.ones((half // 2,), F32)]), 2)
    cos = jnp.cos(a)
    sin = jnp.sin(a) * sign[None, :]
    return jnp.tile(cos, (1, LANES // HEAD_DIM)), jnp.tile(sin, (1, LANES // HEAD_DIM))


def _constants():
    seg = np.kron(np.eye(N_Q_HEADS, dtype=np.float32), np.ones((HEAD_DIM, HEAD_DIM), np.float32))
    def expansion(lane0):
        e = np.zeros((4 * LANES, 2 * GROUP_WIDTH), np.float32)
        for h in range(HEADS_PER_GROUP):
            cols = slice(h * SSM_HEAD_DIM, (h + 1) * SSM_HEAD_DIM)
            e[0 * LANES + lane0 + h, cols] = 1.0
            e[1 * LANES + lane0 + h, cols] = 1.0
            cols2 = slice(GROUP_WIDTH + h * SSM_HEAD_DIM, GROUP_WIDTH + (h + 1) * SSM_HEAD_DIM)
            e[2 * LANES + lane0 + h, cols2] = 1.0
            e[3 * LANES + lane0 + h, cols2] = 1.0
        return e
    tril = np.tril(np.ones((CHUNK, CHUNK), np.float32))
    return dict(
        seg=jnp.asarray(seg, BF16),
        exp_f=jnp.asarray(expansion(0), BF16), exp_b=jnp.asarray(expansion(HEADS_PER_GROUP), BF16),
        tril=jnp.asarray(tril, BF16), triu=jnp.asarray(tril.T, BF16),
    )


def _group_rows(vf, vb):
    return jnp.concatenate([vf.reshape(N_SSM_GROUPS, HEADS_PER_GROUP),
                            vb.reshape(N_SSM_GROUPS, HEADS_PER_GROUP)], axis=1)


def _layer_weights(w_in, dt_bias_f, dt_bias_b, a_log_f, a_log_b):
    wdt = w_in[:, XBC_OFF + CONV_DIM:]
    wf = wdt[:, :N_SSM_HEADS].reshape(D_MODEL, N_SSM_GROUPS, HEADS_PER_GROUP)
    wb = wdt[:, N_SSM_HEADS:].reshape(D_MODEL, N_SSM_GROUPS, HEADS_PER_GROUP)
    grp = jnp.concatenate([wf, wb], axis=-1)
    wdt_pad = jnp.pad(grp, ((0, 0), (0, 0), (0, LANES - DT_ROWS))).reshape(D_MODEL, DT_PAD)
    w_all = jnp.concatenate([w_in[:, :DT_OFF], wdt_pad], axis=1).astype(BF16)
    wdt_t = grp.reshape(D_MODEL, N_SSM_GROUPS * DT_ROWS).T.astype(BF16)
    pad_lanes = lambda v: jnp.pad(v, ((0, 0), (0, LANES - DT_ROWS)))[:, None, :]
    dtb = _group_rows(dt_bias_f, dt_bias_b)
    alog = _group_rows(a_log_f, a_log_b)
    bcast = lambda v: jnp.broadcast_to(v[:, :, None], (N_SSM_GROUPS, DT_ROWS, LANES))
    return w_all, wdt_t, pad_lanes(dtb), pad_lanes(alog), bcast(dtb), bcast(alog)


def kernel(x, ln1_w, w_in, conv_w, conv_b, dt_bias_fwd, dt_bias_bwd, a_log_fwd, a_log_bwd, d_skip,
           ssm_norm_w, q_norm_w, k_norm_w, w_out, ln2_w, w_up, w_down, final_norm_w):
    b, s, _ = x.shape
    depth = w_in.shape[0]
    consts = _constants()
    cos_t, sin_t = _rope_tables(s)
    row = lambda v: v.reshape(1, -1).astype(F32)
    for i in range(depth):
        w_all, wdt_t, dtb, alog, dtb_t, alog_t = _layer_weights(
            w_in[i], dt_bias_fwd[i], dt_bias_bwd[i], a_log_fwd[i], a_log_bwd[i])
        qkv, z, xbc, dt_pad, dt_t = _in_proj(x, row(ln1_w[i]), w_all, wdt_t)
        qt, kn, vt = _qk_prep(
            qkv, cos_t, sin_t, row(jnp.tile(q_norm_w[i], N_Q_HEADS)), row(jnp.tile(k_norm_w[i], N_KV_HEADS)),
            consts["seg"])
        attn = _attention(qt, kn, vt)
        ssm = _ssd(
            xbc, z, dt_pad, dt_t.reshape(b, N_SSM_GROUPS, DT_ROWS, s),
            conv_w[i].astype(F32), row(conv_b[i]), dtb, alog, dtb_t, alog_t,
            row(jnp.repeat(d_skip[i], SSM_HEAD_DIM)), row(ssm_norm_w[i]),
            consts["exp_f"], consts["exp_b"], consts["tril"], consts["triu"])
        x = _out_mlp(
            attn, ssm, x, w_out[i, :ATTN_WIDTH].astype(BF16), w_out[i, ATTN_WIDTH:].astype(BF16),
            row(ln2_w[i]), w_up[i].astype(BF16), w_down[i].astype(BF16), row(final_norm_w),
            final_norm=(i == depth - 1))
    return x
```

```python
import functools

import numpy as np
import jax
import jax.numpy as jnp
from jax import lax
from jax.experimental import pallas as pl
from jax.experimental.pallas import tpu as pltpu

F32 = jnp.float32
BF16 = jnp.bfloat16

D_MODEL = 1024
GRID_W = 64
N_Q_HEADS = 16
N_KV_HEADS = 4
HEAD_DIM = 64
REP = N_Q_HEADS // N_KV_HEADS
ATTN_WIDTH = N_Q_HEADS * HEAD_DIM
KV_WIDTH = N_KV_HEADS * HEAD_DIM
ROPE_THETA = 10000.0
D_INNER = 2048
SSM_HEAD_DIM = 64
N_SSM_HEADS = D_INNER // SSM_HEAD_DIM
N_SSM_GROUPS = 4
HEADS_PER_GROUP = N_SSM_HEADS // N_SSM_GROUPS
GROUP_WIDTH = D_INNER // N_SSM_GROUPS
D_STATE = 128
D_CONV = 5
CHUNK = 128
CONV_DIM = D_INNER + 2 * N_SSM_GROUPS * D_STATE
D_FF = 4 * D_MODEL
NORM_EPS = 1e-5
QK_EPS = 1e-6
LOG2E = 1.4426950408889634

LANES = 128
VMEM_LIMIT_BYTES = 58 * 1024 * 1024

QKV_WIDTH = ATTN_WIDTH + 2 * KV_WIDTH
Z_OFF = QKV_WIDTH
XBC_OFF = Z_OFF + D_INNER
DT_OFF = XBC_OFF + CONV_DIM
DT_PAD = N_SSM_GROUPS * LANES
W_ALL_WIDTH = DT_OFF + DT_PAD
DT_ROWS = 2 * HEADS_PER_GROUP

TM_PROJ = 512
TS_PREP = 512
TQ = 512
KEY_CHUNK = 512
VT_ROWS = HEAD_DIM + 16
TM_MLP = 256
COPY_ROWS = 256
HALO = 16
CONV_STRIP = 256
PLAIN_STRIP = 512
CONV_ROWS = 128


def _dot(a, b):
    return jnp.dot(a, b, preferred_element_type=F32)


def _dot_nt(a, b):
    return lax.dot_general(a, b, (((1,), (1,)), ((), ())), preferred_element_type=F32)


def _params(semantics):
    return pltpu.CompilerParams(dimension_semantics=semantics, vmem_limit_bytes=VMEM_LIMIT_BYTES)


def _const_spec(shape):
    nd = len(shape)
    return pl.BlockSpec(shape, lambda *_: (0,) * nd, pipeline_mode=pl.Buffered(1))


def _silu(v):
    return v * (1.0 / (1.0 + jnp.exp(-v)))


def _inproj_kernel(x_ref, xp_ref, xn_ref, ln_ref, w_ref, wdt_t_ref, cw_ref, cb_ref,
                   qkv_ref, z_ref, xc_ref, bt_ref, dt_ref, dtt_ref, win_scr):
    j = pl.program_id(1)
    tm = x_ref.shape[1]

    def normed(x):
        ms = jnp.mean(x * x, axis=-1, keepdims=True)
        return (x * lax.rsqrt(ms + NORM_EPS) * ln_ref[...]).astype(BF16)

    hn = normed(x_ref[0])
    dtt_ref[0] = _dot_nt(wdt_t_ref[...], hn)

    def plain_job(out_ref, w_off, o_off):
        def run():
            res = _dot(hn, w_ref[:, w_off:w_off + PLAIN_STRIP])
            out_ref[0, :, o_off:o_off + PLAIN_STRIP] = res.astype(out_ref.dtype)
        return run

    plain_jobs = (
        [plain_job(qkv_ref, o, o) for o in range(0, QKV_WIDTH, PLAIN_STRIP)]
        + [plain_job(z_ref, Z_OFF + o, o) for o in range(0, D_INNER, PLAIN_STRIP)]
        + [plain_job(dt_ref, DT_OFF + o, o) for o in range(0, DT_PAD, PLAIN_STRIP)])

    hn_ext = jnp.concatenate([normed(xp_ref[0]), hn, normed(xn_ref[0])], axis=0)
    ext_row = lax.broadcasted_iota(jnp.int32, (tm + 2 * HALO, 1), 0)
    outside = ((ext_row < HALO) & (j == 0)) | ((ext_row >= HALO + tm) & (j == pl.num_programs(1) - 1))
    pad = D_CONV // 2
    n_strips = CONV_DIM // CONV_STRIP
    project = lambda t: _dot(hn_ext, w_ref[:, XBC_OFF + t * CONV_STRIP:XBC_OFF + (t + 1) * CONV_STRIP])
    nxt = project(0)
    for t in range(n_strips):
        cur = nxt
        if t + 1 < n_strips:
            nxt = project(t + 1)
        if plain_jobs:
            plain_jobs.pop(0)()
        win = win_scr.at[t % 2]
        win[...] = jnp.where(outside, 0.0, cur)
        cols = slice(t * CONV_STRIP, (t + 1) * CONV_STRIP)
        b_lo = D_INNER // CONV_STRIP
        b_hi = b_lo + N_SSM_GROUPS * D_STATE // CONV_STRIP
        for r0 in range(0, tm, CONV_ROWS):
            acc = win[HALO - pad + r0:HALO - pad + r0 + CONV_ROWS, :] * cw_ref[0:1, cols] + cb_ref[:, cols]
            for k in range(1, D_CONV):
                off = HALO - pad + k + r0
                acc = acc + win[off:off + CONV_ROWS, :] * cw_ref[k:k + 1, cols]
            y = _silu(acc)
            xc_ref[0, r0:r0 + CONV_ROWS, cols] = y.astype(BF16)
            if b_lo <= t < b_hi:
                bt_ref[0, (t - b_lo) * CONV_STRIP:(t - b_lo + 1) * CONV_STRIP, r0:r0 + CONV_ROWS] = (
                    y.T.astype(BF16))
    for job in plain_jobs:
        job()


def _in_proj(x, ln_w, w_all, wdt_t, conv_w, conv_b):
    b, s, _ = x.shape
    tm = min(TM_PROJ, s)
    grid = (b, s // tm)
    halo_blocks = tm // HALO
    last_halo_block = s // HALO - 1
    tok = lambda width: pl.BlockSpec((1, tm, width), lambda i, j: (i, j, 0))
    bn = N_SSM_GROUPS * D_STATE
    return pl.pallas_call(
        _inproj_kernel,
        grid=grid,
        in_specs=[
            tok(D_MODEL),
            pl.BlockSpec((1, HALO, D_MODEL), lambda i, j: (i, jnp.maximum(j * halo_blocks - 1, 0), 0)),
            pl.BlockSpec((1, HALO, D_MODEL),
                         lambda i, j: (i, jnp.minimum((j + 1) * halo_blocks, last_halo_block), 0)),
            _const_spec((1, D_MODEL)),
            _const_spec((D_MODEL, W_ALL_WIDTH)),
            _const_spec((N_SSM_GROUPS * DT_ROWS, D_MODEL)),
            _const_spec((D_CONV, CONV_DIM)),
            _const_spec((1, CONV_DIM)),
        ],
        out_specs=[
            tok(QKV_WIDTH), tok(D_INNER), tok(CONV_DIM),
            pl.BlockSpec((1, bn, tm), lambda i, j: (i, 0, j)),
            tok(DT_PAD),
            pl.BlockSpec((1, N_SSM_GROUPS * DT_ROWS, tm), lambda i, j: (i, 0, j)),
        ],
        out_shape=[
            jax.ShapeDtypeStruct((b, s, QKV_WIDTH), BF16),
            jax.ShapeDtypeStruct((b, s, D_INNER), BF16),
            jax.ShapeDtypeStruct((b, s, CONV_DIM), BF16),
            jax.ShapeDtypeStruct((b, bn, s), BF16),
            jax.ShapeDtypeStruct((b, s, DT_PAD), F32),
            jax.ShapeDtypeStruct((b, N_SSM_GROUPS * DT_ROWS, s), F32),
        ],
        scratch_shapes=[pltpu.VMEM((2, tm + 2 * HALO, CONV_STRIP), F32)],
        compiler_params=_params(("parallel", "parallel")),
        name="in_proj",
    )(x, x, x, ln_w, w_all, wdt_t, conv_w, conv_b)


def _norm_rope(t, w, seg_ones, cos, sin_signed, post_scale):
    width = t.shape[1]
    ss = _dot((t * t).astype(BF16), seg_ones) * (1.0 / HEAD_DIM)
    y = t * lax.rsqrt(ss + QK_EPS) * w
    reps = width // LANES
    cosw = jnp.concatenate([cos] * reps, axis=1)
    sinw = jnp.concatenate([sin_signed] * reps, axis=1)
    quarter = HEAD_DIM // 4
    upper = pltpu.roll(y, width - quarter, 1)
    lower = pltpu.roll(y, quarter, 1)
    lane = lax.broadcasted_iota(jnp.int32, y.shape, 1)
    first_half = (lane % (2 * quarter)) < quarter
    rot = jnp.where(first_half, upper, lower)
    return (y * cosw + rot * sinw) * post_scale


def _qkprep_kernel(q_ref, k_ref, v_ref, cos_ref, sin_ref, qw_ref, kw_ref, seg_ref,
                   qt_ref, kn_ref, vt_ref):
    cos = cos_ref[...]
    sin = sin_ref[...]
    q = _norm_rope(q_ref[0].astype(F32), qw_ref[...], seg_ref[...], cos, sin,
                   (HEAD_DIM ** -0.5) * LOG2E)
    qt_ref[0] = q.T.astype(BF16)
    k = _norm_rope(k_ref[0].astype(F32), kw_ref[...], seg_ref[0:KV_WIDTH, 0:KV_WIDTH], cos, sin, 1.0)
    kn_ref[0] = k.astype(BF16)
    vt = v_ref[0].astype(F32).T.astype(BF16)
    ones = jnp.ones((VT_ROWS - HEAD_DIM, vt.shape[1]), BF16)
    vt_ref[0] = jnp.concatenate(
        [piece for g in range(N_KV_HEADS) for piece in (vt[g * HEAD_DIM:(g + 1) * HEAD_DIM], ones)], axis=0)


def _qk_prep(qkv, cos_t, sin_t, qw, kw, seg_ones):
    b, s, _ = qkv.shape
    ts = min(TS_PREP, s)
    grid = (b, s // ts)
    return pl.pallas_call(
        _qkprep_kernel,
        grid=grid,
        in_specs=[
            pl.BlockSpec((1, ts, ATTN_WIDTH), lambda i, j: (i, j, 0)),
            pl.BlockSpec((1, ts, KV_WIDTH), lambda i, j: (i, j, ATTN_WIDTH // KV_WIDTH)),
            pl.BlockSpec((1, ts, KV_WIDTH), lambda i, j: (i, j, ATTN_WIDTH // KV_WIDTH + 1)),
            pl.BlockSpec((ts, LANES), lambda i, j: (j, 0)),
            pl.BlockSpec((ts, LANES), lambda i, j: (j, 0)),
            _const_spec((1, ATTN_WIDTH)),
            _const_spec((1, KV_WIDTH)),
            _const_spec((ATTN_WIDTH, ATTN_WIDTH)),
        ],
        out_specs=[
            pl.BlockSpec((1, ATTN_WIDTH, ts), lambda i, j: (i, 0, j)),
            pl.BlockSpec((1, ts, KV_WIDTH), lambda i, j: (i, j, 0)),
            pl.BlockSpec((1, N_KV_HEADS * VT_ROWS, ts), lambda i, j: (i, 0, j)),
        ],
        out_shape=[
            jax.ShapeDtypeStruct((b, ATTN_WIDTH, s), BF16),
            jax.ShapeDtypeStruct((b, s, KV_WIDTH), BF16),
            jax.ShapeDtypeStruct((b, N_KV_HEADS * VT_ROWS, s), BF16),
        ],
        compiler_params=_params(("parallel", "parallel")),
        name="qk_prep",
    )(qkv, qkv, qkv, cos_t, sin_t, qw, kw, seg_ones)


def _attn_kernel(qt_ref, k_ref, vt_ref, o_ref, qe_scr):
    g = pl.program_id(1)
    tq = qt_ref.shape[2]
    s_len = k_ref.shape[1]
    kc = min(KEY_CHUNK, s_len)
    qt = qt_ref[0]
    q_cols = jnp.concatenate([qt[r * HEAD_DIM:(r + 1) * HEAD_DIM, :] for r in range(REP)], axis=1)
    qe_scr[...] = jnp.zeros_like(qe_scr)
    qe_scr[pl.ds(pl.multiple_of(g * HEAD_DIM, HEAD_DIM), HEAD_DIM), :] = q_cols
    qe = qe_scr[...]
    m = jnp.full((1, REP * tq), -jnp.inf, F32)
    acc = jnp.zeros((VT_ROWS, REP * tq), F32)
    n_kc = s_len // kc
    scores = lambda c: _dot(k_ref[0, c * kc:(c + 1) * kc, :], qe)
    st_next = scores(0)
    for c in range(n_kc):
        st = st_next
        if c + 1 < n_kc:
            st_next = scores(c + 1)
        m_new = jnp.maximum(m, jnp.max(st, axis=0, keepdims=True))
        alpha = jnp.exp2(m - m_new)
        p = jnp.exp2(st - m_new)
        acc = alpha * acc + _dot(vt_ref[0, :, c * kc:(c + 1) * kc], p.astype(BF16))
        m = m_new
    ot = (acc[:HEAD_DIM] / acc[HEAD_DIM:HEAD_DIM + 1]).T
    out = jnp.concatenate([ot[r * tq:(r + 1) * tq, :] for r in range(REP)], axis=1)
    o_ref[0] = out.astype(o_ref.dtype)


def _attention(qt, kn, vt):
    b, _, s = qt.shape
    tq = min(TQ, s)
    gw = REP * HEAD_DIM
    grid = (b, N_KV_HEADS, s // tq)
    return pl.pallas_call(
        _attn_kernel,
        grid=grid,
        in_specs=[
            pl.BlockSpec((1, gw, tq), lambda i, g, j: (i, g, j)),
            pl.BlockSpec((1, s, KV_WIDTH), lambda i, g, j: (i, 0, 0)),
            pl.BlockSpec((1, VT_ROWS, s), lambda i, g, j: (i, g, 0)),
        ],
        out_specs=pl.BlockSpec((1, tq, gw), lambda i, g, j: (i, j, g)),
        out_shape=jax.ShapeDtypeStruct((b, s, ATTN_WIDTH), BF16),
        scratch_shapes=[pltpu.VMEM((KV_WIDTH, REP * tq), BF16)],
        compiler_params=_params(("parallel", "parallel", "arbitrary")),
        name="attention",
    )(qt, kn, vt)


def _split_hi_lo(v):
    hi = v.astype(BF16)
    lo = (v - hi.astype(F32)).astype(BF16)
    return hi, lo


def _softplus(v):
    return jnp.maximum(v, 0.0) + jnp.log1p(jnp.exp(-jnp.abs(v)))


def _ssd_kernel(xc_ref, b_ref, bt_ref, c_ref, dt_ref, dtt_ref,
                dtb_ref, alog_ref, dtbt_ref, alogt_ref, dskip_ref,
                expf_ref, expb_ref, tril_ref, triu_ref,
                o_ref,
                y_scr, stf_scr, stb_scr):
    s_len = xc_ref.shape[1]
    n_chunks = s_len // CHUNK
    hp = HEADS_PER_GROUP
    rows = min(COPY_ROWS, s_len)

    def skip_body(i, carry):
        r0 = pl.multiple_of(i * rows, rows)
        y_scr[pl.ds(r0, rows), :] = dskip_ref[...] * xc_ref[0, pl.ds(r0, rows), :].astype(F32)
        return carry

    lax.fori_loop(0, s_len // rows, skip_body, 0)

    row = lax.broadcasted_iota(jnp.int32, (CHUNK, CHUNK), 0)
    col = lax.broadcasted_iota(jnp.int32, (CHUNK, CHUNK), 1)
    lane = lax.broadcasted_iota(jnp.int32, (CHUNK, LANES), 1)
    lane_lt_half = lane < SSM_HEAD_DIM
    neg_a = -jnp.exp(alog_ref[0]) * LOG2E
    neg_a_t = -jnp.exp(alogt_ref[0]) * LOG2E

    def chunk_step(forward, c, st_scr):
        tri = tril_ref[...] if forward else triu_ref[...]
        tri_t = triu_ref[...] if forward else tril_ref[...]
        keep = (row >= col) if forward else (row <= col)
        expand = expf_ref[...] if forward else expb_ref[...]
        lane0 = 0 if forward else hp
        last = CHUNK - 1 if forward else 0
        r0 = pl.multiple_of(c * CHUNK, CHUNK)
        bc = b_ref[0, pl.ds(r0, CHUNK), :]
        bt = bt_ref[0, :, pl.ds(r0, CHUNK)]
        cc = c_ref[0, pl.ds(r0, CHUNK), :]
        dts = _softplus(dt_ref[0, pl.ds(r0, CHUNK), :] + dtb_ref[0])
        a_hi, a_lo = _split_hi_lo(dts * neg_a)
        acs2 = _dot(tri, jnp.concatenate([a_hi, a_lo], axis=1))
        dts_t = _softplus(dtt_ref[0, 0, :, pl.ds(r0, CHUNK)] + dtbt_ref[0])
        at_hi, at_lo = _split_hi_lo(dts_t * neg_a_t)
        acst2 = _dot(jnp.concatenate([at_hi, at_lo], axis=0), tri_t)
        cb = _dot_nt(cc, bc)
        yield
        acs = acs2[:, :LANES] + acs2[:, LANES:]
        acs_t = acst2[:DT_ROWS] + acst2[DT_ROWS:]
        d_hi, d_lo = _split_hi_lo(dts)
        c_hi, c_lo = _split_hi_lo(acs)
        packed = jnp.concatenate([d_hi, d_lo, c_hi, c_lo], axis=1)
        ex = _dot(packed, expand)
        ms = []
        for j in range(hp // 2):
            pair = []
            for h in (2 * j, 2 * j + 1):
                seg = acs[:, lane0 + h:lane0 + h + 1] - acs_t[lane0 + h:lane0 + h + 1, :]
                lmat = jnp.exp2(jnp.where(keep, seg, -jnp.inf))
                pair.append((cb * lmat).astype(BF16))
            ms.append(jnp.concatenate(pair, axis=1))
        yield
        dt_exp = ex[:, :GROUP_WIDTH]
        acs_exp = ex[:, GROUP_WIDTH:]
        acs_last = acs_exp[last:last + 1, :]
        xdt = xc_ref[0, pl.ds(r0, CHUNK), :].astype(F32) * dt_exp
        xdt_b = xdt.astype(BF16)
        xd = (xdt * jnp.exp2(acs_last - acs_exp)).astype(BF16)
        state = st_scr[...]
        y_off = _dot(cc, state.astype(BF16))
        st_new = _dot(bt, xd)
        pairs = []
        for j in range(hp // 2):
            xp = xdt_b[:, j * LANES:(j + 1) * LANES]
            zero = jnp.zeros_like(xp)
            rhs = jnp.concatenate([jnp.where(lane_lt_half, xp, zero),
                                   jnp.where(lane_lt_half, zero, xp)], axis=0)
            pairs.append(_dot(ms[j], rhs))
        yield
        st_scr[...] = state * jnp.exp2(acs_last) + st_new
        y_scr[pl.ds(r0, CHUNK), :] += y_off * jnp.exp2(acs_exp) + jnp.concatenate(pairs, axis=1)

    def scan_body(i, carry):
        active = [chunk_step(True, i, stf_scr), chunk_step(False, n_chunks - 1 - i, stb_scr)]
        while active:
            active = [gen for gen in active if next(gen, True) is None]
        return carry

    stf_scr[...] = jnp.zeros_like(stf_scr)
    stb_scr[...] = jnp.zeros_like(stb_scr)
    lax.fori_loop(0, n_chunks, scan_body, 0)

    def out_body(i, carry):
        r0 = pl.multiple_of(i * rows, rows)
        o_ref[0, pl.ds(r0, rows), :] = y_scr[pl.ds(r0, rows), :].astype(o_ref.dtype)
        return carry

    lax.fori_loop(0, s_len // rows, out_body, 0)


def _ssd(xc, bt, dt_pad, dt_t, dtb, alog, dtb_t, alog_t, dskip, exp_f, exp_b, tril, triu):
    b, s, _ = xc.shape
    grid = (b, N_SSM_GROUPS)
    gw = GROUP_WIDTH
    b_block0 = D_INNER // D_STATE
    c_block0 = b_block0 + N_SSM_GROUPS
    seq = lambda width, off: pl.BlockSpec((1, s, width), lambda i, g: (i, 0, off + g))
    gspec = lambda rows: pl.BlockSpec((1, rows, LANES), lambda i, g: (g, 0, 0))
    return pl.pallas_call(
        _ssd_kernel,
        grid=grid,
        in_specs=[
            seq(gw, 0), seq(D_STATE, b_block0),
            pl.BlockSpec((1, D_STATE, s), lambda i, g: (i, g, 0)),
            seq(D_STATE, c_block0),
            seq(LANES, 0),
            pl.BlockSpec((1, 1, DT_ROWS, s), lambda i, g: (i, g, 0, 0)),
            gspec(1), gspec(1), gspec(DT_ROWS), gspec(DT_ROWS),
            pl.BlockSpec((1, gw), lambda i, g: (0, g)),
            _const_spec((4 * LANES, 2 * gw)), _const_spec((4 * LANES, 2 * gw)),
            _const_spec((CHUNK, CHUNK)), _const_spec((CHUNK, CHUNK)),
        ],
        out_specs=pl.BlockSpec((1, s, gw), lambda i, g: (i, 0, g)),
        out_shape=jax.ShapeDtypeStruct((b, s, D_INNER), BF16),
        scratch_shapes=[
            pltpu.VMEM((s, gw), F32),
            pltpu.VMEM((D_STATE, gw), F32),
            pltpu.VMEM((D_STATE, gw), F32),
        ],
        compiler_params=_params(("parallel", "arbitrary")),
        name="ssd",
    )(xc, xc, bt, xc, dt_pad, dt_t, dtb, alog, dtb_t, alog_t, dskip, exp_f, exp_b, tril, triu)


def _outmlp_kernel(attn_ref, y_ref, z_ref, x_ref, nw_ref, woa_ref, wos_ref, ln2_ref, wup_ref, wdn_ref,
                   fin_ref, o_ref, *, final_norm):
    gated = []
    for gi in range(N_SSM_GROUPS):
        cols = slice(gi * GROUP_WIDTH, (gi + 1) * GROUP_WIDTH)
        yg = y_ref[0, :, cols].astype(F32) * _silu(z_ref[0, :, cols].astype(F32))
        msg = jnp.mean(yg * yg, axis=-1, keepdims=True)
        gated.append((yg * lax.rsqrt(msg + NORM_EPS) * nw_ref[:, cols]).astype(BF16))
    ssm = jnp.concatenate(gated, axis=1)
    x1 = x_ref[0] + _dot(attn_ref[0], woa_ref[...]) + _dot(ssm, wos_ref[...])
    ms = jnp.mean(x1 * x1, axis=-1, keepdims=True)
    h = (x1 * lax.rsqrt(ms + NORM_EPS) * ln2_ref[...]).astype(BF16)
    u = jnp.maximum(_dot(h, wup_ref[...]), 0.0)
    x2 = x1 + _dot((u * u).astype(BF16), wdn_ref[...])
    if final_norm:
        ms2 = jnp.mean(x2 * x2, axis=-1, keepdims=True)
        x2 = x2 * lax.rsqrt(ms2 + NORM_EPS) * fin_ref[...]
    o_ref[0] = x2


def _out_mlp(attn, y, z, x, norm_w, wo_attn, wo_ssm, ln2, w_up, w_down, fin_w, final_norm):
    b, s, _ = x.shape
    tm = min(TM_MLP, s)
    grid = (b, s // tm)
    tok = lambda width: pl.BlockSpec((1, tm, width), lambda i, j: (i, j, 0))
    return pl.pallas_call(
        functools.partial(_outmlp_kernel, final_norm=final_norm),
        grid=grid,
        in_specs=[
            tok(ATTN_WIDTH), tok(D_INNER), tok(D_INNER), tok(D_MODEL),
            _const_spec((1, D_INNER)),
            _const_spec((ATTN_WIDTH, D_MODEL)), _const_spec((D_INNER, D_MODEL)),
            _const_spec((1, D_MODEL)),
            _const_spec((D_MODEL, D_FF)), _const_spec((D_FF, D_MODEL)),
            _const_spec((1, D_MODEL)),
        ],
        out_specs=tok(D_MODEL),
        out_shape=jax.ShapeDtypeStruct((b, s, D_MODEL), F32),
        compiler_params=_params(("parallel", "parallel")),
        name="out_mlp",
    )(attn, y, z, x, norm_w, wo_attn, wo_ssm, ln2, w_up, w_down, fin_w)


def _rope_tables(seq):
    rows = seq // GRID_W
    row_ids = jnp.repeat(jnp.arange(rows, dtype=jnp.int32), GRID_W)
    col_ids = jnp.tile(jnp.arange(GRID_W, dtype=jnp.int32), rows)
    half = HEAD_DIM // 2
    inv_freq = ROPE_THETA ** (-jnp.arange(0, half, 2, dtype=F32) / half)

    def ang(pos):
        a = pos.astype(F32)[:, None] * inv_freq[None, :]
        return jnp.concatenate([a, a], axis=-1)

    a = jnp.concatenate([ang(row_ids), ang(col_ids)], axis=-1)
    sign = jnp.tile(jnp.concatenate([-jnp.ones((half // 2,), F32), jnp.ones((half // 2,), F32)]), 2)
    cos = jnp.cos(a)
    sin = jnp.sin(a) * sign[None, :]
    return jnp.tile(cos, (1, LANES // HEAD_DIM)), jnp.tile(sin, (1, LANES // HEAD_DIM))


def _constants():
    seg = np.kron(np.eye(N_Q_HEADS, dtype=np.float32), np.ones((HEAD_DIM, HEAD_DIM), np.float32))
    def expansion(lane0):
        e = np.zeros((4 * LANES, 2 * GROUP_WIDTH), np.float32)
        for h in range(HEADS_PER_GROUP):
            cols = slice(h * SSM_HEAD_DIM, (h + 1) * SSM_HEAD_DIM)
            e[0 * LANES + lane0 + h, cols] = 1.0
            e[1 * LANES + lane0 + h, cols] = 1.0
            cols2 = slice(GROUP_WIDTH + h * SSM_HEAD_DIM, GROUP_WIDTH + (h + 1) * SSM_HEAD_DIM)
            e[2 * LANES + lane0 + h, cols2] = 1.0
            e[3 * LANES + lane0 + h, cols2] = 1.0
        return e
    tril = np.tril(np.ones((CHUNK, CHUNK), np.float32))
    return dict(
        seg=jnp.asarray(seg, BF16),
        exp_f=jnp.asarray(expansion(0), BF16), exp_b=jnp.asarray(expansion(HEADS_PER_GROUP), BF16),
        tril=jnp.asarray(tril, BF16), triu=jnp.asarray(tril.T, BF16),
    )


def _group_rows(vf, vb):
    return jnp.concatenate([vf.reshape(N_SSM_GROUPS, HEADS_PER_GROUP),
                            vb.reshape(N_SSM_GROUPS, HEADS_PER_GROUP)], axis=1)


def _layer_weights(w_in, dt_bias_f, dt_bias_b, a_log_f, a_log_b):
    wdt = w_in[:, XBC_OFF + CONV_DIM:]
    wf = wdt[:, :N_SSM_HEADS].reshape(D_MODEL, N_SSM_GROUPS, HEADS_PER_GROUP)
    wb = wdt[:, N_SSM_HEADS:].reshape(D_MODEL, N_SSM_GROUPS, HEADS_PER_GROUP)
    grp = jnp.concatenate([wf, wb], axis=-1)
    wdt_pad = jnp.pad(grp, ((0, 0), (0, 0), (0, LANES - DT_ROWS))).reshape(D_MODEL, DT_PAD)
    w_all = jnp.concatenate([w_in[:, :DT_OFF], wdt_pad], axis=1).astype(BF16)
    wdt_t = grp.reshape(D_MODEL, N_SSM_GROUPS * DT_ROWS).T.astype(BF16)
    pad_lanes = lambda v: jnp.pad(v, ((0, 0), (0, LANES - DT_ROWS)))[:, None, :]
    dtb = _group_rows(dt_bias_f, dt_bias_b)
    alog = _group_rows(a_log_f, a_log_b)
    bcast = lambda v: jnp.broadcast_to(v[:, :, None], (N_SSM_GROUPS, DT_ROWS, LANES))
    return w_all, wdt_t, pad_lanes(dtb), pad_lanes(alog), bcast(dtb), bcast(alog)


def kernel(x, ln1_w, w_in, conv_w, conv_b, dt_bias_fwd, dt_bias_bwd, a_log_fwd, a_log_bwd, d_skip,
           ssm_norm_w, q_norm_w, k_norm_w, w_out, ln2_w, w_up, w_down, final_norm_w):
    b, s, _ = x.shape
    depth = w_in.shape[0]
    consts = _constants()
    cos_t, sin_t = _rope_tables(s)
    row = lambda v: v.reshape(1, -1).astype(F32)
    for i in range(depth):
        w_all, wdt_t, dtb, alog, dtb_t, alog_t = _layer_weights(
            w_in[i], dt_bias_fwd[i], dt_bias_bwd[i], a_log_fwd[i], a_log_bwd[i])
        qkv, z, xc, bt, dt_pad, dt_t = _in_proj(
            x, row(ln1_w[i]), w_all, wdt_t, conv_w[i].astype(F32), row(conv_b[i]))
        qt, kn, vt = _qk_prep(
            qkv, cos_t, sin_t, row(jnp.tile(q_norm_w[i], N_Q_HEADS)), row(jnp.tile(k_norm_w[i], N_KV_HEADS)),
            consts["seg"])
        attn = _attention(qt, kn, vt)
        y = _ssd(
            xc, bt, dt_pad, dt_t.reshape(b, N_SSM_GROUPS, DT_ROWS, s), dtb, alog, dtb_t, alog_t,
            row(jnp.repeat(d_skip[i], SSM_HEAD_DIM)),
            consts["exp_f"], consts["exp_b"], consts["tril"], consts["triu"])
        x = _out_mlp(
            attn, y, z, x, row(ssm_norm_w[i]),
            w_out[i, :ATTN_WIDTH].astype(BF16), w_out[i, ATTN_WIDTH:].astype(BF16),
            row(ln2_w[i]), w_up[i].astype(BF16), w_down[i].astype(BF16), row(final_norm_w),
            final_norm=(i == depth - 1))
    return x
```

```python
import functools

import numpy as np
import jax
import jax.numpy as jnp
from jax import lax
from jax.experimental import pallas as pl
from jax.experimental.pallas import tpu as pltpu

F32 = jnp.float32
BF16 = jnp.bfloat16

D_MODEL = 1024
GRID_W = 64
N_Q_HEADS = 16
N_KV_HEADS = 4
HEAD_DIM = 64
REP = N_Q_HEADS // N_KV_HEADS
ATTN_WIDTH = N_Q_HEADS * HEAD_DIM
KV_WIDTH = N_KV_HEADS * HEAD_DIM
ROPE_THETA = 10000.0
D_INNER = 2048
SSM_HEAD_DIM = 64
N_SSM_HEADS = D_INNER // SSM_HEAD_DIM
N_SSM_GROUPS = 4
HEADS_PER_GROUP = N_SSM_HEADS // N_SSM_GROUPS
GROUP_WIDTH = D_INNER // N_SSM_GROUPS
D_STATE = 128
D_CONV = 5
CHUNK = 128
CONV_DIM = D_INNER + 2 * N_SSM_GROUPS * D_STATE
D_FF = 4 * D_MODEL
NORM_EPS = 1e-5
QK_EPS = 1e-6
LOG2E = 1.4426950408889634

LANES = 128
VMEM_LIMIT_BYTES = 58 * 1024 * 1024

QKV_WIDTH = ATTN_WIDTH + 2 * KV_WIDTH
Z_OFF = QKV_WIDTH
XBC_OFF = Z_OFF + D_INNER
DT_OFF = XBC_OFF + CONV_DIM
DT_PAD = N_SSM_GROUPS * LANES
W_ALL_WIDTH = DT_OFF + DT_PAD
DT_ROWS = 2 * HEADS_PER_GROUP

TM_PROJ = 512
TS_PREP = 512
TQ = 512
KEY_CHUNK = 512
VT_ROWS = HEAD_DIM + 16
TM_MLP = 256
COPY_ROWS = 256
HALO = 16
CONV_STRIP = 256
PLAIN_STRIP = 512
CONV_ROWS = 128


def _dot(a, b):
    return jnp.dot(a, b, preferred_element_type=F32)


def _dot_nt(a, b):
    return lax.dot_general(a, b, (((1,), (1,)), ((), ())), preferred_element_type=F32)


def _params(semantics, flags=None):
    return pltpu.CompilerParams(dimension_semantics=semantics, vmem_limit_bytes=VMEM_LIMIT_BYTES, flags=flags)


def _const_spec(shape):
    nd = len(shape)
    return pl.BlockSpec(shape, lambda *_: (0,) * nd, pipeline_mode=pl.Buffered(1))


def _silu(v):
    return v * (1.0 / (1.0 + jnp.exp(-v)))


def _inproj_kernel(x_ref, xp_ref, xn_ref, ln_ref, w_ref, wdt_t_ref, cw_ref, cb_ref,
                   qkv_ref, z_ref, xc_ref, bt_ref, dt_ref, dtt_ref, win_scr):
    j = pl.program_id(1)
    tm = x_ref.shape[1]

    def normed(x):
        ms = jnp.mean(x * x, axis=-1, keepdims=True)
        return (x * lax.rsqrt(ms + NORM_EPS) * ln_ref[...]).astype(BF16)

    hn = normed(x_ref[0])
    dtt_ref[0] = _dot_nt(wdt_t_ref[...], hn)

    def plain_job(out_ref, w_off, o_off):
        def run():
            res = _dot(hn, w_ref[:, w_off:w_off + PLAIN_STRIP])
            out_ref[0, :, o_off:o_off + PLAIN_STRIP] = res.astype(out_ref.dtype)
        return run

    plain_jobs = (
        [plain_job(qkv_ref, o, o) for o in range(0, QKV_WIDTH, PLAIN_STRIP)]
        + [plain_job(z_ref, Z_OFF + o, o) for o in range(0, D_INNER, PLAIN_STRIP)]
        + [plain_job(dt_ref, DT_OFF + o, o) for o in range(0, DT_PAD, PLAIN_STRIP)])

    hn_ext = jnp.concatenate([normed(xp_ref[0]), hn, normed(xn_ref[0])], axis=0)
    ext_row = lax.broadcasted_iota(jnp.int32, (tm + 2 * HALO, 1), 0)
    outside = ((ext_row < HALO) & (j == 0)) | ((ext_row >= HALO + tm) & (j == pl.num_programs(1) - 1))
    pad = D_CONV // 2
    n_strips = CONV_DIM // CONV_STRIP
    project = lambda t: _dot(hn_ext, w_ref[:, XBC_OFF + t * CONV_STRIP:XBC_OFF + (t + 1) * CONV_STRIP])
    nxt = project(0)
    for t in range(n_strips):
        cur = nxt
        if t + 1 < n_strips:
            nxt = project(t + 1)
        if plain_jobs:
            plain_jobs.pop(0)()
        win = win_scr.at[t % 2]
        win[...] = jnp.where(outside, 0.0, cur)
        cols = slice(t * CONV_STRIP, (t + 1) * CONV_STRIP)
        b_lo = D_INNER // CONV_STRIP
        b_hi = b_lo + N_SSM_GROUPS * D_STATE // CONV_STRIP
        for r0 in range(0, tm, CONV_ROWS):
            acc = win[HALO - pad + r0:HALO - pad + r0 + CONV_ROWS, :] * cw_ref[0:1, cols] + cb_ref[:, cols]
            for k in range(1, D_CONV):
                off = HALO - pad + k + r0
                acc = acc + win[off:off + CONV_ROWS, :] * cw_ref[k:k + 1, cols]
            y = _silu(acc)
            xc_ref[0, r0:r0 + CONV_ROWS, cols] = y.astype(BF16)
            if b_lo <= t < b_hi:
                bt_ref[0, (t - b_lo) * CONV_STRIP:(t - b_lo + 1) * CONV_STRIP, r0:r0 + CONV_ROWS] = (
                    y.T.astype(BF16))
    for job in plain_jobs:
        job()


def _in_proj(x, ln_w, w_all, wdt_t, conv_w, conv_b):
    b, s, _ = x.shape
    tm = min(TM_PROJ, s)
    grid = (b, s // tm)
    halo_blocks = tm // HALO
    last_halo_block = s // HALO - 1
    tok = lambda width: pl.BlockSpec((1, tm, width), lambda i, j: (i, j, 0))
    bn = N_SSM_GROUPS * D_STATE
    return pl.pallas_call(
        _inproj_kernel,
        grid=grid,
        in_specs=[
            tok(D_MODEL),
            pl.BlockSpec((1, HALO, D_MODEL), lambda i, j: (i, jnp.maximum(j * halo_blocks - 1, 0), 0)),
            pl.BlockSpec((1, HALO, D_MODEL),
                         lambda i, j: (i, jnp.minimum((j + 1) * halo_blocks, last_halo_block), 0)),
            _const_spec((1, D_MODEL)),
            _const_spec((D_MODEL, W_ALL_WIDTH)),
            _const_spec((N_SSM_GROUPS * DT_ROWS, D_MODEL)),
            _const_spec((D_CONV, CONV_DIM)),
            _const_spec((1, CONV_DIM)),
        ],
        out_specs=[
            tok(QKV_WIDTH), tok(D_INNER), tok(CONV_DIM),
            pl.BlockSpec((1, bn, tm), lambda i, j: (i, 0, j)),
            tok(DT_PAD),
            pl.BlockSpec((1, N_SSM_GROUPS * DT_ROWS, tm), lambda i, j: (i, 0, j)),
        ],
        out_shape=[
            jax.ShapeDtypeStruct((b, s, QKV_WIDTH), BF16),
            jax.ShapeDtypeStruct((b, s, D_INNER), BF16),
            jax.ShapeDtypeStruct((b, s, CONV_DIM), BF16),
            jax.ShapeDtypeStruct((b, bn, s), BF16),
            jax.ShapeDtypeStruct((b, s, DT_PAD), F32),
            jax.ShapeDtypeStruct((b, N_SSM_GROUPS * DT_ROWS, s), F32),
        ],
        scratch_shapes=[pltpu.VMEM((2, tm + 2 * HALO, CONV_STRIP), F32)],
        compiler_params=_params(("parallel", "parallel")),
        name="in_proj",
    )(x, x, x, ln_w, w_all, wdt_t, conv_w, conv_b)


def _norm_rope(t, w, seg_ones, cos, sin_signed, post_scale):
    width = t.shape[1]
    ss = _dot((t * t).astype(BF16), seg_ones) * (1.0 / HEAD_DIM)
    y = t * lax.rsqrt(ss + QK_EPS) * w
    reps = width // LANES
    cosw = jnp.concatenate([cos] * reps, axis=1)
    sinw = jnp.concatenate([sin_signed] * reps, axis=1)
    quarter = HEAD_DIM // 4
    upper = pltpu.roll(y, width - quarter, 1)
    lower = pltpu.roll(y, quarter, 1)
    lane = lax.broadcasted_iota(jnp.int32, y.shape, 1)
    first_half = (lane % (2 * quarter)) < quarter
    rot = jnp.where(first_half, upper, lower)
    return (y * cosw + rot * sinw) * post_scale


def _qkprep_kernel(q_ref, k_ref, v_ref, cos_ref, sin_ref, qw_ref, kw_ref, seg_ref,
                   qt_ref, kn_ref, vt_ref):
    cos = cos_ref[...]
    sin = sin_ref[...]
    q = _norm_rope(q_ref[0].astype(F32), qw_ref[...], seg_ref[...], cos, sin,
                   (HEAD_DIM ** -0.5) * LOG2E)
    qt_ref[0] = q.T.astype(BF16)
    k = _norm_rope(k_ref[0].astype(F32), kw_ref[...], seg_ref[0:KV_WIDTH, 0:KV_WIDTH], cos, sin, 1.0)
    kn_ref[0] = k.astype(BF16)
    vt = v_ref[0].astype(F32).T.astype(BF16)
    ones = jnp.ones((VT_ROWS - HEAD_DIM, vt.shape[1]), BF16)
    vt_ref[0] = jnp.concatenate(
        [piece for g in range(N_KV_HEADS) for piece in (vt[g * HEAD_DIM:(g + 1) * HEAD_DIM], ones)], axis=0)


def _qk_prep(qkv, cos_t, sin_t, qw, kw, seg_ones):
    b, s, _ = qkv.shape
    ts = min(TS_PREP, s)
    grid = (b, s // ts)
    return pl.pallas_call(
        _qkprep_kernel,
        grid=grid,
        in_specs=[
            pl.BlockSpec((1, ts, ATTN_WIDTH), lambda i, j: (i, j, 0)),
            pl.BlockSpec((1, ts, KV_WIDTH), lambda i, j: (i, j, ATTN_WIDTH // KV_WIDTH)),
            pl.BlockSpec((1, ts, KV_WIDTH), lambda i, j: (i, j, ATTN_WIDTH // KV_WIDTH + 1)),
            pl.BlockSpec((ts, LANES), lambda i, j: (j, 0)),
            pl.BlockSpec((ts, LANES), lambda i, j: (j, 0)),
            _const_spec((1, ATTN_WIDTH)),
            _const_spec((1, KV_WIDTH)),
            _const_spec((ATTN_WIDTH, ATTN_WIDTH)),
        ],
        out_specs=[
            pl.BlockSpec((1, ATTN_WIDTH, ts), lambda i, j: (i, 0, j)),
            pl.BlockSpec((1, ts, KV_WIDTH), lambda i, j: (i, j, 0)),
            pl.BlockSpec((1, N_KV_HEADS * VT_ROWS, ts), lambda i, j: (i, 0, j)),
        ],
        out_shape=[
            jax.ShapeDtypeStruct((b, ATTN_WIDTH, s), BF16),
            jax.ShapeDtypeStruct((b, s, KV_WIDTH), BF16),
            jax.ShapeDtypeStruct((b, N_KV_HEADS * VT_ROWS, s), BF16),
        ],
        compiler_params=_params(("parallel", "parallel")),
        name="qk_prep",
    )(qkv, qkv, qkv, cos_t, sin_t, qw, kw, seg_ones)


def _attn_stages(qt_ref, k_ref, vt_ref, o_ref, qe_scr):
    g = pl.program_id(1)
    tq = qt_ref.shape[2]
    s_len = k_ref.shape[1]
    kc = min(KEY_CHUNK, s_len)
    qt = qt_ref[0]
    q_cols = jnp.concatenate([qt[r * HEAD_DIM:(r + 1) * HEAD_DIM, :] for r in range(REP)], axis=1)
    qe_scr[...] = jnp.zeros_like(qe_scr)
    qe_scr[pl.ds(pl.multiple_of(g * HEAD_DIM, HEAD_DIM), HEAD_DIM), :] = q_cols
    qe = qe_scr[...]
    m = jnp.full((1, REP * tq), -jnp.inf, F32)
    acc = jnp.zeros((VT_ROWS, REP * tq), F32)
    n_kc = s_len // kc
    scores = lambda c: _dot(k_ref[0, c * kc:(c + 1) * kc, :], qe)
    st_next = scores(0)
    pending = None
    for c in range(n_kc):
        st = st_next
        if c + 1 < n_kc:
            st_next = scores(c + 1)
        if pending is not None:
            alpha_p, p_p, c_p = pending
            acc = alpha_p * acc + _dot(vt_ref[0, :, c_p * kc:(c_p + 1) * kc], p_p)
        yield
        m_new = jnp.maximum(m, jnp.max(st, axis=0, keepdims=True))
        alpha = jnp.exp2(m - m_new)
        pending = (alpha, jnp.exp2(st - m_new).astype(BF16), c)
        m = m_new
        yield
    alpha_p, p_p, c_p = pending
    acc = alpha_p * acc + _dot(vt_ref[0, :, c_p * kc:(c_p + 1) * kc], p_p)
    ot = (acc[:HEAD_DIM] / acc[HEAD_DIM:HEAD_DIM + 1]).T
    out = jnp.concatenate([ot[r * tq:(r + 1) * tq, :] for r in range(REP)], axis=1)
    o_ref[0] = out.astype(o_ref.dtype)


def _split_hi_lo(v):
    hi = v.astype(BF16)
    lo = (v - hi.astype(F32)).astype(BF16)
    return hi, lo


def _softplus(v):
    return jnp.maximum(v, 0.0) + jnp.log1p(jnp.exp(-jnp.abs(v)))


def _mixer_kernel(qt_ref, k_ref, vt_ref,
                  xc_ref, b_ref, bt_ref, c_ref, dt_ref, dtt_ref,
                  dtb_ref, alog_ref, dtbt_ref, alogt_ref, dskip_ref,
                  tril_ref, triu_ref,
                  attn_ref, y_ref,
                  qe_scr, y_scr, stf_scr, stb_scr):
    j = pl.program_id(2)
    n_blocks = pl.num_programs(2)
    s_len = xc_ref.shape[1]
    n_chunks = s_len // CHUNK
    iters = n_chunks // (s_len // qt_ref.shape[2])
    hp = HEADS_PER_GROUP
    rows = min(COPY_ROWS, s_len)

    @pl.when(j == 0)
    def _():
        def skip_body(i, carry):
            r0 = pl.multiple_of(i * rows, rows)
            y_scr[pl.ds(r0, rows), :] = dskip_ref[...] * xc_ref[0, pl.ds(r0, rows), :].astype(F32)
            return carry

        lax.fori_loop(0, s_len // rows, skip_body, 0)
        stf_scr[...] = jnp.zeros_like(stf_scr)
        stb_scr[...] = jnp.zeros_like(stb_scr)

    row = lax.broadcasted_iota(jnp.int32, (CHUNK, CHUNK), 0)
    col = lax.broadcasted_iota(jnp.int32, (CHUNK, CHUNK), 1)
    lane = lax.broadcasted_iota(jnp.int32, (CHUNK, LANES), 1)
    lane_lt_half = lane < SSM_HEAD_DIM
    neg_a = -jnp.exp(alog_ref[0]) * LOG2E
    neg_a_t = -jnp.exp(alogt_ref[0]) * LOG2E

    def chunk_step(forward, c, st_scr):
        tri = tril_ref[...] if forward else triu_ref[...]
        tri_t = triu_ref[...] if forward else tril_ref[...]
        keep = (row >= col) if forward else (row <= col)
        lane0 = 0 if forward else hp
        last = CHUNK - 1 if forward else 0
        r0 = pl.multiple_of(c * CHUNK, CHUNK)
        bc = b_ref[0, pl.ds(r0, CHUNK), :]
        bt = bt_ref[0, :, pl.ds(r0, CHUNK)]
        cc = c_ref[0, pl.ds(r0, CHUNK), :]
        dts = _softplus(dt_ref[0, pl.ds(r0, CHUNK), :] + dtb_ref[0])
        a_hi, a_lo = _split_hi_lo(dts * neg_a)
        acs2 = _dot(tri, jnp.concatenate([a_hi, a_lo], axis=1))
        dts_t = _softplus(dtt_ref[0, 0, :, pl.ds(r0, CHUNK)] + dtbt_ref[0])
        at_hi, at_lo = _split_hi_lo(dts_t * neg_a_t)
        acst2 = _dot(jnp.concatenate([at_hi, at_lo], axis=0), tri_t)
        cb = _dot_nt(cc, bc)
        yield
        acs = acs2[:, :LANES] + acs2[:, LANES:]
        acs_t = acst2[:DT_ROWS] + acst2[DT_ROWS:]
        ms, dt_tiles, acs_tiles = [], [], []
        for j in range(hp // 2):
            pair, dt_cols, acs_cols = [], [], []
            for h in (2 * j, 2 * j + 1):
                acs_col = jnp.broadcast_to(acs[:, lane0 + h:lane0 + h + 1], (CHUNK, LANES))
                dt_cols.append(jnp.broadcast_to(dts[:, lane0 + h:lane0 + h + 1], (CHUNK, LANES)))
                acs_cols.append(acs_col)
                seg = acs_col - acs_t[lane0 + h:lane0 + h + 1, :]
                lmat = jnp.exp2(jnp.where(keep, seg, -jnp.inf))
                pair.append((cb * lmat).astype(BF16))
            ms.append(jnp.concatenate(pair, axis=1))
            dt_tiles.append(jnp.where(lane_lt_half, dt_cols[0], dt_cols[1]))
            acs_tiles.append(jnp.where(lane_lt_half, acs_cols[0], acs_cols[1]))
        yield
        dt_exp = jnp.concatenate(dt_tiles, axis=1)
        acs_exp = jnp.concatenate(acs_tiles, axis=1)
        acs_last = acs_exp[last:last + 1, :]
        xdt = xc_ref[0, pl.ds(r0, CHUNK), :].astype(F32) * dt_exp
        xdt_b = xdt.astype(BF16)
        xd = (xdt * jnp.exp2(acs_last - acs_exp)).astype(BF16)
        state = st_scr[...]
        y_off = _dot(cc, state.astype(BF16))
        st_new = _dot(bt, xd)
        pairs = []
        for j in range(hp // 2):
            xp = xdt_b[:, j * LANES:(j + 1) * LANES]
            zero = jnp.zeros_like(xp)
            rhs = jnp.concatenate([jnp.where(lane_lt_half, xp, zero),
                                   jnp.where(lane_lt_half, zero, xp)], axis=0)
            pairs.append(_dot(ms[j], rhs))
        yield
        st_scr[...] = state * jnp.exp2(acs_last) + st_new
        y_scr[pl.ds(r0, CHUNK), :] += y_off * jnp.exp2(acs_exp) + jnp.concatenate(pairs, axis=1)

    def scan_stages():
        for it in range(iters):
            i = j * iters + it
            active = [chunk_step(True, i, stf_scr), chunk_step(False, n_chunks - 1 - i, stb_scr)]
            while active:
                active = [gen for gen in active if next(gen, True) is None]
                yield

    streams = [_attn_stages(qt_ref, k_ref, vt_ref, attn_ref, qe_scr), scan_stages()]
    while streams:
        streams = [gen for gen in streams if next(gen, True) is None]

    @pl.when(j == n_blocks - 1)
    def _():
        def out_body(i, carry):
            r0 = pl.multiple_of(i * rows, rows)
            y_ref[0, pl.ds(r0, rows), :] = y_scr[pl.ds(r0, rows), :].astype(y_ref.dtype)
            return carry

        lax.fori_loop(0, s_len // rows, out_body, 0)


def _mixer(qt, kn, vt, xc, bt, dt_pad, dt_t, dtb, alog, dtb_t, alog_t, dskip, tril, triu):
    b, s, _ = xc.shape
    tq = min(TQ, s)
    assert (s // CHUNK) % (s // tq) == 0
    grid = (b, N_SSM_GROUPS, s // tq)
    gw = GROUP_WIDTH
    aw = REP * HEAD_DIM
    b_block0 = D_INNER // D_STATE
    c_block0 = b_block0 + N_SSM_GROUPS
    once = pl.Buffered(1)
    seq = lambda width, off: pl.BlockSpec((1, s, width), lambda i, g, j: (i, 0, off + g), pipeline_mode=once)
    gspec = lambda rows: pl.BlockSpec((1, rows, LANES), lambda i, g, j: (g, 0, 0))
    return pl.pallas_call(
        _mixer_kernel,
        grid=grid,
        in_specs=[
            pl.BlockSpec((1, aw, tq), lambda i, g, j: (i, g, j)),
            pl.BlockSpec((1, s, KV_WIDTH), lambda i, g, j: (i, 0, 0)),
            pl.BlockSpec((1, VT_ROWS, s), lambda i, g, j: (i, g, 0)),
            seq(gw, 0), seq(D_STATE, b_block0),
            pl.BlockSpec((1, D_STATE, s), lambda i, g, j: (i, g, 0), pipeline_mode=once),
            seq(D_STATE, c_block0),
            seq(LANES, 0),
            pl.BlockSpec((1, 1, DT_ROWS, s), lambda i, g, j: (i, g, 0, 0)),
            gspec(1), gspec(1), gspec(DT_ROWS), gspec(DT_ROWS),
            pl.BlockSpec((1, gw), lambda i, g, j: (0, g)),
            _const_spec((CHUNK, CHUNK)), _const_spec((CHUNK, CHUNK)),
        ],
        out_specs=[
            pl.BlockSpec((1, tq, aw), lambda i, g, j: (i, j, g)),
            pl.BlockSpec((1, s, gw), lambda i, g, j: (i, 0, g)),
        ],
        out_shape=[
            jax.ShapeDtypeStruct((b, s, ATTN_WIDTH), BF16),
            jax.ShapeDtypeStruct((b, s, D_INNER), BF16),
        ],
        scratch_shapes=[
            pltpu.VMEM((KV_WIDTH, REP * tq), BF16),
            pltpu.VMEM((s, gw), F32),
            pltpu.VMEM((D_STATE, gw), F32),
            pltpu.VMEM((D_STATE, gw), F32),
        ],
        compiler_params=_params(("parallel", "arbitrary", "arbitrary")),
        name="mixer",
    )(qt, kn, vt, xc, xc, bt, xc, dt_pad, dt_t, dtb, alog, dtb_t, alog_t, dskip, tril, triu)


def _outmlp_kernel(attn_ref, y_ref, z_ref, x_ref, nw_ref, woa_ref, wos_ref, ln2_ref, wup_ref, wdn_ref,
                   fin_ref, o_ref, *, final_norm):
    gated = []
    for gi in range(N_SSM_GROUPS):
        cols = slice(gi * GROUP_WIDTH, (gi + 1) * GROUP_WIDTH)
        yg = y_ref[0, :, cols].astype(F32) * _silu(z_ref[0, :, cols].astype(F32))
        msg = jnp.mean(yg * yg, axis=-1, keepdims=True)
        gated.append((yg * lax.rsqrt(msg + NORM_EPS) * nw_ref[:, cols]).astype(BF16))
    ssm = jnp.concatenate(gated, axis=1)
    x1 = x_ref[0] + _dot(attn_ref[0], woa_ref[...]) + _dot(ssm, wos_ref[...])
    ms = jnp.mean(x1 * x1, axis=-1, keepdims=True)
    h = (x1 * lax.rsqrt(ms + NORM_EPS) * ln2_ref[...]).astype(BF16)
    u = jnp.maximum(_dot(h, wup_ref[...]), 0.0)
    x2 = x1 + _dot((u * u).astype(BF16), wdn_ref[...])
    if final_norm:
        ms2 = jnp.mean(x2 * x2, axis=-1, keepdims=True)
        x2 = x2 * lax.rsqrt(ms2 + NORM_EPS) * fin_ref[...]
    o_ref[0] = x2


def _out_mlp(attn, y, z, x, norm_w, wo_attn, wo_ssm, ln2, w_up, w_down, fin_w, final_norm):
    b, s, _ = x.shape
    tm = min(TM_MLP, s)
    grid = (b, s // tm)
    tok = lambda width: pl.BlockSpec((1, tm, width), lambda i, j: (i, j, 0))
    return pl.pallas_call(
        functools.partial(_outmlp_kernel, final_norm=final_norm),
        grid=grid,
        in_specs=[
            tok(ATTN_WIDTH), tok(D_INNER), tok(D_INNER), tok(D_MODEL),
            _const_spec((1, D_INNER)),
            _const_spec((ATTN_WIDTH, D_MODEL)), _const_spec((D_INNER, D_MODEL)),
            _const_spec((1, D_MODEL)),
            _const_spec((D_MODEL, D_FF)), _const_spec((D_FF, D_MODEL)),
            _const_spec((1, D_MODEL)),
        ],
        out_specs=tok(D_MODEL),
        out_shape=jax.ShapeDtypeStruct((b, s, D_MODEL), F32),
        compiler_params=_params(("parallel", "parallel")),
        name="out_mlp",
    )(attn, y, z, x, norm_w, wo_attn, wo_ssm, ln2, w_up, w_down, fin_w)


def _rope_tables(seq):
    rows = seq // GRID_W
    row_ids = jnp.repeat(jnp.arange(rows, dtype=jnp.int32), GRID_W)
    col_ids = jnp.tile(jnp.arange(GRID_W, dtype=jnp.int32), rows)
    half = HEAD_DIM // 2
    inv_freq = ROPE_THETA ** (-jnp.arange(0, half, 2, dtype=F32) / half)

    def ang(pos):
        a = pos.astype(F32)[:, None] * inv_freq[None, :]
        return jnp.concatenate([a, a], axis=-1)

    a = jnp.concatenate([ang(row_ids), ang(col_ids)], axis=-1)
    sign = jnp.tile(jnp.concatenate([-jnp.ones((half // 2,), F32), jnp.ones((half // 2,), F32)]), 2)
    cos = jnp.cos(a)
    sin = jnp.sin(a) * sign[None, :]
    return jnp.tile(cos, (1, LANES // HEAD_DIM)), jnp.tile(sin, (1, LANES // HEAD_DIM))


def _constants():
    seg = np.kron(np.eye(N_Q_HEADS, dtype=np.float32), np.ones((HEAD_DIM, HEAD_DIM), np.float32))
    tril = np.tril(np.ones((CHUNK, CHUNK), np.float32))
    return dict(
        seg=jnp.asarray(seg, BF16),
        tril=jnp.asarray(tril, BF16), triu=jnp.asarray(tril.T, BF16),
    )


def _group_rows(vf, vb):
    return jnp.concatenate([vf.reshape(N_SSM_GROUPS, HEADS_PER_GROUP),
                            vb.reshape(N_SSM_GROUPS, HEADS_PER_GROUP)], axis=1)


def _layer_weights(w_in, dt_bias_f, dt_bias_b, a_log_f, a_log_b):
    wdt = w_in[:, XBC_OFF + CONV_DIM:]
    wf = wdt[:, :N_SSM_HEADS].reshape(D_MODEL, N_SSM_GROUPS, HEADS_PER_GROUP)
    wb = wdt[:, N_SSM_HEADS:].reshape(D_MODEL, N_SSM_GROUPS, HEADS_PER_GROUP)
    grp = jnp.concatenate([wf, wb], axis=-1)
    wdt_pad = jnp.pad(grp, ((0, 0), (0, 0), (0, LANES - DT_ROWS))).reshape(D_MODEL, DT_PAD)
    w_all = jnp.concatenate([w_in[:, :DT_OFF], wdt_pad], axis=1).astype(BF16)
    wdt_t = grp.reshape(D_MODEL, N_SSM_GROUPS * DT_ROWS).T.astype(BF16)
    pad_lanes = lambda v: jnp.pad(v, ((0, 0), (0, LANES - DT_ROWS)))[:, None, :]
    dtb = _group_rows(dt_bias_f, dt_bias_b)
    alog = _group_rows(a_log_f, a_log_b)
    bcast = lambda v: jnp.broadcast_to(v[:, :, None], (N_SSM_GROUPS, DT_ROWS, LANES))
    return w_all, wdt_t, pad_lanes(dtb), pad_lanes(alog), bcast(dtb), bcast(alog)


def kernel(x, ln1_w, w_in, conv_w, conv_b, dt_bias_fwd, dt_bias_bwd, a_log_fwd, a_log_bwd, d_skip,
           ssm_norm_w, q_norm_w, k_norm_w, w_out, ln2_w, w_up, w_down, final_norm_w):
    b, s, _ = x.shape
    depth = w_in.shape[0]
    consts = _constants()
    cos_t, sin_t = _rope_tables(s)
    row = lambda v: v.reshape(1, -1).astype(F32)
    for i in range(depth):
        w_all, wdt_t, dtb, alog, dtb_t, alog_t = _layer_weights(
            w_in[i], dt_bias_fwd[i], dt_bias_bwd[i], a_log_fwd[i], a_log_bwd[i])
        qkv, z, xc, bt, dt_pad, dt_t = _in_proj(
            x, row(ln1_w[i]), w_all, wdt_t, conv_w[i].astype(F32), row(conv_b[i]))
        qt, kn, vt = _qk_prep(
            qkv, cos_t, sin_t, row(jnp.tile(q_norm_w[i], N_Q_HEADS)), row(jnp.tile(k_norm_w[i], N_KV_HEADS)),
            consts["seg"])
        attn, y = _mixer(
            qt, kn, vt, xc, bt, dt_pad, dt_t.reshape(b, N_SSM_GROUPS, DT_ROWS, s), dtb, alog, dtb_t, alog_t,
            row(jnp.repeat(d_skip[i], SSM_HEAD_DIM)), consts["tril"], consts["triu"])
        x = _out_mlp(
            attn, y, z, x, row(ssm_norm_w[i]),
            w_out[i, :ATTN_WIDTH].astype(BF16), w_out[i, ATTN_WIDTH:].astype(BF16),
            row(ln2_w[i]), w_up[i].astype(BF16), w_down[i].astype(BF16), row(final_norm_w),
            final_norm=(i == depth - 1))
    return x
```

```python
import functools

import numpy as np
import jax
import jax.numpy as jnp
from jax import lax
from jax.experimental import pallas as pl
from jax.experimental.pallas import tpu as pltpu

F32 = jnp.float32
BF16 = jnp.bfloat16

D_MODEL = 1024
GRID_W = 64
N_Q_HEADS = 16
N_KV_HEADS = 4
HEAD_DIM = 64
REP = N_Q_HEADS // N_KV_HEADS
ATTN_WIDTH = N_Q_HEADS * HEAD_DIM
KV_WIDTH = N_KV_HEADS * HEAD_DIM
ROPE_THETA = 10000.0
D_INNER = 2048
SSM_HEAD_DIM = 64
N_SSM_HEADS = D_INNER // SSM_HEAD_DIM
N_SSM_GROUPS = 4
HEADS_PER_GROUP = N_SSM_HEADS // N_SSM_GROUPS
GROUP_WIDTH = D_INNER // N_SSM_GROUPS
D_STATE = 128
D_CONV = 5
CHUNK = 128
CONV_DIM = D_INNER + 2 * N_SSM_GROUPS * D_STATE
D_FF = 4 * D_MODEL
NORM_EPS = 1e-5
QK_EPS = 1e-6
LOG2E = 1.4426950408889634

LANES = 128
VMEM_LIMIT_BYTES = 58 * 1024 * 1024

QKV_WIDTH = ATTN_WIDTH + 2 * KV_WIDTH
Z_OFF = QKV_WIDTH
XBC_OFF = Z_OFF + D_INNER
DT_OFF = XBC_OFF + CONV_DIM
DT_PAD = N_SSM_GROUPS * LANES
W_ALL_WIDTH = DT_OFF + DT_PAD
DT_ROWS = 2 * HEADS_PER_GROUP

TM_PROJ = 512
TS_PREP = 512
TQ = 512
KEY_CHUNK = 512
VT_ROWS = HEAD_DIM + 16
SAFE_SHIFT = 60.0
TM_MLP = 256
COPY_ROWS = 256
HALO = 16
CONV_STRIP = 256
PLAIN_STRIP = 512
CONV_ROWS = 128


def _dot(a, b):
    return jnp.dot(a, b, preferred_element_type=F32)


def _dot_nt(a, b):
    return lax.dot_general(a, b, (((1,), (1,)), ((), ())), preferred_element_type=F32)


def _params(semantics, flags=None):
    return pltpu.CompilerParams(dimension_semantics=semantics, vmem_limit_bytes=VMEM_LIMIT_BYTES, flags=flags)


def _const_spec(shape):
    nd = len(shape)
    return pl.BlockSpec(shape, lambda *_: (0,) * nd, pipeline_mode=pl.Buffered(1))


def _silu(v):
    return v * (1.0 / (1.0 + jnp.exp(-v)))


def _inproj_kernel(x_ref, xp_ref, xn_ref, ln_ref, w_ref, wdt_t_ref, cw_ref, cb_ref,
                   qkv_ref, z_ref, xc_ref, bt_ref, dt_ref, dtt_ref, win_scr):
    j = pl.program_id(1)
    tm = x_ref.shape[1]

    def normed(x):
        ms = jnp.mean(x * x, axis=-1, keepdims=True)
        return (x * lax.rsqrt(ms + NORM_EPS) * ln_ref[...]).astype(BF16)

    hn = normed(x_ref[0])
    dtt_ref[0] = _dot_nt(wdt_t_ref[...], hn)

    def plain_job(out_ref, w_off, o_off):
        def run():
            res = _dot(hn, w_ref[:, w_off:w_off + PLAIN_STRIP])
            out_ref[0, :, o_off:o_off + PLAIN_STRIP] = res.astype(out_ref.dtype)
        return run

    plain_jobs = (
        [plain_job(qkv_ref, o, o) for o in range(0, QKV_WIDTH, PLAIN_STRIP)]
        + [plain_job(z_ref, Z_OFF + o, o) for o in range(0, D_INNER, PLAIN_STRIP)]
        + [plain_job(dt_ref, DT_OFF + o, o) for o in range(0, DT_PAD, PLAIN_STRIP)])

    hn_ext = jnp.concatenate([normed(xp_ref[0]), hn, normed(xn_ref[0])], axis=0)
    ext_row = lax.broadcasted_iota(jnp.int32, (tm + 2 * HALO, 1), 0)
    outside = ((ext_row < HALO) & (j == 0)) | ((ext_row >= HALO + tm) & (j == pl.num_programs(1) - 1))
    pad = D_CONV // 2
    n_strips = CONV_DIM // CONV_STRIP
    project = lambda t: _dot(hn_ext, w_ref[:, XBC_OFF + t * CONV_STRIP:XBC_OFF + (t + 1) * CONV_STRIP])
    nxt = project(0)
    for t in range(n_strips):
        cur = nxt
        if t + 1 < n_strips:
            nxt = project(t + 1)
        if plain_jobs:
            plain_jobs.pop(0)()
        win = win_scr.at[t % 2]
        win[...] = jnp.where(outside, 0.0, cur)
        cols = slice(t * CONV_STRIP, (t + 1) * CONV_STRIP)
        b_lo = D_INNER // CONV_STRIP
        b_hi = b_lo + N_SSM_GROUPS * D_STATE // CONV_STRIP
        for r0 in range(0, tm, CONV_ROWS):
            acc = win[HALO - pad + r0:HALO - pad + r0 + CONV_ROWS, :] * cw_ref[0:1, cols] + cb_ref[:, cols]
            for k in range(1, D_CONV):
                off = HALO - pad + k + r0
                acc = acc + win[off:off + CONV_ROWS, :] * cw_ref[k:k + 1, cols]
            y = _silu(acc)
            xc_ref[0, r0:r0 + CONV_ROWS, cols] = y.astype(BF16)
            if b_lo <= t < b_hi:
                bt_ref[0, (t - b_lo) * CONV_STRIP:(t - b_lo + 1) * CONV_STRIP, r0:r0 + CONV_ROWS] = (
                    y.T.astype(BF16))
    for job in plain_jobs:
        job()


def _in_proj(x, ln_w, w_all, wdt_t, conv_w, conv_b):
    b, s, _ = x.shape
    tm = min(TM_PROJ, s)
    grid = (b, s // tm)
    halo_blocks = tm // HALO
    last_halo_block = s // HALO - 1
    tok = lambda width: pl.BlockSpec((1, tm, width), lambda i, j: (i, j, 0))
    bn = N_SSM_GROUPS * D_STATE
    return pl.pallas_call(
        _inproj_kernel,
        grid=grid,
        in_specs=[
            tok(D_MODEL),
            pl.BlockSpec((1, HALO, D_MODEL), lambda i, j: (i, jnp.maximum(j * halo_blocks - 1, 0), 0)),
            pl.BlockSpec((1, HALO, D_MODEL),
                         lambda i, j: (i, jnp.minimum((j + 1) * halo_blocks, last_halo_block), 0)),
            _const_spec((1, D_MODEL)),
            _const_spec((D_MODEL, W_ALL_WIDTH)),
            _const_spec((N_SSM_GROUPS * DT_ROWS, D_MODEL)),
            _const_spec((D_CONV, CONV_DIM)),
            _const_spec((1, CONV_DIM)),
        ],
        out_specs=[
            tok(QKV_WIDTH), tok(D_INNER), tok(CONV_DIM),
            pl.BlockSpec((1, bn, tm), lambda i, j: (i, 0, j)),
            tok(DT_PAD),
            pl.BlockSpec((1, N_SSM_GROUPS * DT_ROWS, tm), lambda i, j: (i, 0, j)),
        ],
        out_shape=[
            jax.ShapeDtypeStruct((b, s, QKV_WIDTH), BF16),
            jax.ShapeDtypeStruct((b, s, D_INNER), BF16),
            jax.ShapeDtypeStruct((b, s, CONV_DIM), BF16),
            jax.ShapeDtypeStruct((b, bn, s), BF16),
            jax.ShapeDtypeStruct((b, s, DT_PAD), F32),
            jax.ShapeDtypeStruct((b, N_SSM_GROUPS * DT_ROWS, s), F32),
        ],
        scratch_shapes=[pltpu.VMEM((2, tm + 2 * HALO, CONV_STRIP), F32)],
        compiler_params=_params(("parallel", "parallel")),
        name="in_proj",
    )(x, x, x, ln_w, w_all, wdt_t, conv_w, conv_b)


def _norm_rope(t, w, seg_ones, cos, sin_signed, post_scale):
    width = t.shape[1]
    ss = _dot((t * t).astype(BF16), seg_ones) * (1.0 / HEAD_DIM)
    y = t * lax.rsqrt(ss + QK_EPS) * w
    reps = width // LANES
    cosw = jnp.concatenate([cos] * reps, axis=1)
    sinw = jnp.concatenate([sin_signed] * reps, axis=1)
    quarter = HEAD_DIM // 4
    upper = pltpu.roll(y, width - quarter, 1)
    lower = pltpu.roll(y, quarter, 1)
    lane = lax.broadcasted_iota(jnp.int32, y.shape, 1)
    first_half = (lane % (2 * quarter)) < quarter
    rot = jnp.where(first_half, upper, lower)
    return (y * cosw + rot * sinw) * post_scale


def _qkprep_kernel(q_ref, k_ref, v_ref, cos_ref, sin_ref, qw_ref, kw_ref, seg_ref,
                   qt_ref, kn_ref, vt_ref):
    cos = cos_ref[...]
    sin = sin_ref[...]
    q = _norm_rope(q_ref[0].astype(F32), qw_ref[...], seg_ref[...], cos, sin,
                   (HEAD_DIM ** -0.5) * LOG2E)
    qt_ref[0] = q.T.astype(BF16)
    k = _norm_rope(k_ref[0].astype(F32), kw_ref[...], seg_ref[0:KV_WIDTH, 0:KV_WIDTH], cos, sin, 1.0)
    kn_ref[0] = k.astype(BF16)
    vt = v_ref[0].astype(F32).T.astype(BF16)
    ones = jnp.ones((VT_ROWS - HEAD_DIM, vt.shape[1]), BF16)
    vt_ref[0] = jnp.concatenate(
        [piece for g in range(N_KV_HEADS) for piece in (vt[g * HEAD_DIM:(g + 1) * HEAD_DIM], ones)], axis=0)


def _qk_prep(qkv, cos_t, sin_t, qw, kw, seg_ones):
    b, s, _ = qkv.shape
    ts = min(TS_PREP, s)
    grid = (b, s // ts)
    return pl.pallas_call(
        _qkprep_kernel,
        grid=grid,
        in_specs=[
            pl.BlockSpec((1, ts, ATTN_WIDTH), lambda i, j: (i, j, 0)),
            pl.BlockSpec((1, ts, KV_WIDTH), lambda i, j: (i, j, ATTN_WIDTH // KV_WIDTH)),
            pl.BlockSpec((1, ts, KV_WIDTH), lambda i, j: (i, j, ATTN_WIDTH // KV_WIDTH + 1)),
            pl.BlockSpec((ts, LANES), lambda i, j: (j, 0)),
            pl.BlockSpec((ts, LANES), lambda i, j: (j, 0)),
            _const_spec((1, ATTN_WIDTH)),
            _const_spec((1, KV_WIDTH)),
            _const_spec((ATTN_WIDTH, ATTN_WIDTH)),
        ],
        out_specs=[
            pl.BlockSpec((1, ATTN_WIDTH, ts), lambda i, j: (i, 0, j)),
            pl.BlockSpec((1, ts, KV_WIDTH), lambda i, j: (i, j, 0)),
            pl.BlockSpec((1, N_KV_HEADS * VT_ROWS, ts), lambda i, j: (i, 0, j)),
        ],
        out_shape=[
            jax.ShapeDtypeStruct((b, ATTN_WIDTH, s), BF16),
            jax.ShapeDtypeStruct((b, s, KV_WIDTH), BF16),
            jax.ShapeDtypeStruct((b, N_KV_HEADS * VT_ROWS, s), BF16),
        ],
        compiler_params=_params(("parallel", "parallel")),
        name="qk_prep",
    )(qkv, qkv, qkv, cos_t, sin_t, qw, kw, seg_ones)


def _attn_setup(qt_ref, k_ref, bound_ref, qe_scr, shift_scr):
    g = pl.program_id(1)
    s_len = k_ref.shape[1]
    kc = min(KEY_CHUNK, s_len)
    qt = qt_ref[0]
    q_cols = jnp.concatenate([qt[r * HEAD_DIM:(r + 1) * HEAD_DIM, :] for r in range(REP)], axis=1)
    qe_scr[...] = jnp.zeros_like(qe_scr)
    qe_scr[pl.ds(pl.multiple_of(g * HEAD_DIM, HEAD_DIM), HEAD_DIM), :] = q_cols
    bound = bound_ref[0, 0]

    @pl.when(bound <= SAFE_SHIFT)
    def _():
        shift_scr[...] = jnp.full(shift_scr.shape, bound, F32)

    @pl.when(bound > SAFE_SHIFT)
    def _():
        def body(c, mx):
            r0 = pl.multiple_of(c * kc, kc)
            st = _dot(k_ref[0, pl.ds(r0, kc), :], qe_scr[...])
            return jnp.maximum(mx, jnp.max(st, axis=0, keepdims=True))

        shift_scr[...] = lax.fori_loop(0, s_len // kc, body, jnp.full(shift_scr.shape, -jnp.inf, F32))


def _attn_stages(k_ref, vt_ref, o_ref, qe_scr, shift_scr, p_scr):
    tq = o_ref.shape[1]
    s_len = k_ref.shape[1]
    kc = min(KEY_CHUNK, s_len)
    n_kc = s_len // kc
    n_cols = REP * tq
    acc = jnp.zeros((VT_ROWS, n_cols), F32)
    scores = lambda c: _dot(k_ref[0, c * kc:(c + 1) * kc, :], qe_scr[...])
    st_next = scores(0)
    for c in range(n_kc + 1):
        st = st_next
        if c + 1 < n_kc:
            st_next = scores(c + 1)
        if c > 0:
            acc = acc + _dot(vt_ref[0, :, (c - 1) * kc:c * kc], p_scr[(c - 1) % 2])
        yield
        if c < n_kc:
            p_scr[c % 2] = jnp.exp2(st - shift_scr[...]).astype(BF16)
            yield
    ot = (acc[:HEAD_DIM] / acc[HEAD_DIM:HEAD_DIM + 1]).T
    out = jnp.concatenate([ot[r * tq:(r + 1) * tq, :] for r in range(REP)], axis=1)
    o_ref[0] = out.astype(o_ref.dtype)


def _split_hi_lo(v):
    hi = v.astype(BF16)
    lo = (v - hi.astype(F32)).astype(BF16)
    return hi, lo


def _softplus(v):
    return jnp.maximum(v, 0.0) + jnp.log1p(jnp.exp(-jnp.abs(v)))


def _mixer_kernel(bound_ref, qt_ref, k_ref, vt_ref,
                  xc_ref, b_ref, bt_ref, c_ref, dt_ref, dtt_ref,
                  dtb_ref, alog_ref, dtbt_ref, alogt_ref, dskip_ref,
                  tril_ref, triu_ref,
                  attn_ref, y_ref,
                  qe_scr, shift_scr, p_scr, y_scr, stf_scr, stb_scr):
    _attn_setup(qt_ref, k_ref, bound_ref, qe_scr, shift_scr)
    j = pl.program_id(2)
    n_blocks = pl.num_programs(2)
    s_len = xc_ref.shape[1]
    n_chunks = s_len // CHUNK
    iters = n_chunks // (s_len // qt_ref.shape[2])
    hp = HEADS_PER_GROUP
    rows = min(COPY_ROWS, s_len)

    @pl.when(j == 0)
    def _():
        def skip_body(i, carry):
            r0 = pl.multiple_of(i * rows, rows)
            y_scr[pl.ds(r0, rows), :] = dskip_ref[...] * xc_ref[0, pl.ds(r0, rows), :].astype(F32)
            return carry

        lax.fori_loop(0, s_len // rows, skip_body, 0)
        stf_scr[...] = jnp.zeros_like(stf_scr)
        stb_scr[...] = jnp.zeros_like(stb_scr)

    row = lax.broadcasted_iota(jnp.int32, (CHUNK, CHUNK), 0)
    col = lax.broadcasted_iota(jnp.int32, (CHUNK, CHUNK), 1)
    lane = lax.broadcasted_iota(jnp.int32, (CHUNK, LANES), 1)
    lane_lt_half = lane < SSM_HEAD_DIM
    neg_a = -jnp.exp(alog_ref[0]) * LOG2E
    neg_a_t = -jnp.exp(alogt_ref[0]) * LOG2E

    def chunk_step(forward, c, st_scr):
        tri = tril_ref[...] if forward else triu_ref[...]
        tri_t = triu_ref[...] if forward else tril_ref[...]
        keep = (row >= col) if forward else (row <= col)
        lane0 = 0 if forward else hp
        last = CHUNK - 1 if forward else 0
        r0 = pl.multiple_of(c * CHUNK, CHUNK)
        bc = b_ref[0, pl.ds(r0, CHUNK), :]
        bt = bt_ref[0, :, pl.ds(r0, CHUNK)]
        cc = c_ref[0, pl.ds(r0, CHUNK), :]
        dts = _softplus(dt_ref[0, pl.ds(r0, CHUNK), :] + dtb_ref[0])
        a_hi, a_lo = _split_hi_lo(dts * neg_a)
        a_cat = jnp.concatenate([a_hi, a_lo], axis=1)
        dts_t = _softplus(dtt_ref[0, 0, :, pl.ds(r0, CHUNK)] + dtbt_ref[0])
        at_hi, at_lo = _split_hi_lo(dts_t * neg_a_t)
        at_cat = jnp.concatenate([at_hi, at_lo], axis=0)
        yield
        acs2 = _dot(tri, a_cat)
        acst2 = _dot(at_cat, tri_t)
        cb = _dot_nt(cc, bc)
        yield
        acs = acs2[:, :LANES] + acs2[:, LANES:]
        acs_t = acst2[:DT_ROWS] + acst2[DT_ROWS:]
        ms, dt_tiles, acs_tiles = [], [], []
        for j in range(hp // 2):
            pair, dt_cols, acs_cols = [], [], []
            for h in (2 * j, 2 * j + 1):
                acs_col = jnp.broadcast_to(acs[:, lane0 + h:lane0 + h + 1], (CHUNK, LANES))
                dt_cols.append(jnp.broadcast_to(dts[:, lane0 + h:lane0 + h + 1], (CHUNK, LANES)))
                acs_cols.append(acs_col)
                seg = acs_col - acs_t[lane0 + h:lane0 + h + 1, :]
                lmat = jnp.exp2(jnp.where(keep, seg, -jnp.inf))
                pair.append((cb * lmat).astype(BF16))
            ms.append(jnp.concatenate(pair, axis=1))
            dt_tiles.append(jnp.where(lane_lt_half, dt_cols[0], dt_cols[1]))
            acs_tiles.append(jnp.where(lane_lt_half, acs_cols[0], acs_cols[1]))
        dt_exp = jnp.concatenate(dt_tiles, axis=1)
        acs_exp = jnp.concatenate(acs_tiles, axis=1)
        acs_last = acs_exp[last:last + 1, :]
        xdt = xc_ref[0, pl.ds(r0, CHUNK), :].astype(F32) * dt_exp
        xdt_b = xdt.astype(BF16)
        xd = (xdt * jnp.exp2(acs_last - acs_exp)).astype(BF16)
        rhs = []
        for j in range(hp // 2):
            xp = xdt_b[:, j * LANES:(j + 1) * LANES]
            zero = jnp.zeros_like(xp)
            rhs.append(jnp.concatenate([jnp.where(lane_lt_half, xp, zero),
                                        jnp.where(lane_lt_half, zero, xp)], axis=0))
        yield
        state = st_scr[...]
        y_off = _dot(cc, state.astype(BF16))
        st_new = _dot(bt, xd)
        pairs = [_dot(ms[j], rhs[j]) for j in range(hp // 2)]
        yield
        st_scr[...] = state * jnp.exp2(acs_last) + st_new
        y_scr[pl.ds(r0, CHUNK), :] += y_off * jnp.exp2(acs_exp) + jnp.concatenate(pairs, axis=1)

    def scan_stages():
        for it in range(iters):
            i = j * iters + it
            active = [chunk_step(True, i, stf_scr), chunk_step(False, n_chunks - 1 - i, stb_scr)]
            while active:
                active = [gen for gen in active if next(gen, True) is None]
                yield

    streams = [_attn_stages(k_ref, vt_ref, attn_ref, qe_scr, shift_scr, p_scr), scan_stages()]
    while streams:
        streams = [gen for gen in streams if next(gen, True) is None]

    @pl.when(j == n_blocks - 1)
    def _():
        def out_body(i, carry):
            r0 = pl.multiple_of(i * rows, rows)
            y_ref[0, pl.ds(r0, rows), :] = y_scr[pl.ds(r0, rows), :].astype(y_ref.dtype)
            return carry

        lax.fori_loop(0, s_len // rows, out_body, 0)


def _mixer(score_bound, qt, kn, vt, xc, bt, dt_pad, dt_t, dtb, alog, dtb_t, alog_t, dskip, tril, triu):
    b, s, _ = xc.shape
    tq = min(TQ, s)
    assert (s // CHUNK) % (s // tq) == 0
    grid = (b, N_SSM_GROUPS, s // tq)
    gw = GROUP_WIDTH
    aw = REP * HEAD_DIM
    b_block0 = D_INNER // D_STATE
    c_block0 = b_block0 + N_SSM_GROUPS
    once = pl.Buffered(1)
    seq = lambda width, off: pl.BlockSpec((1, s, width), lambda i, g, j: (i, 0, off + g), pipeline_mode=once)
    gspec = lambda rows: pl.BlockSpec((1, rows, LANES), lambda i, g, j: (g, 0, 0))
    return pl.pallas_call(
        _mixer_kernel,
        grid=grid,
        in_specs=[
            pl.BlockSpec(memory_space=pltpu.SMEM),
            pl.BlockSpec((1, aw, tq), lambda i, g, j: (i, g, j)),
            pl.BlockSpec((1, s, KV_WIDTH), lambda i, g, j: (i, 0, 0)),
            pl.BlockSpec((1, VT_ROWS, s), lambda i, g, j: (i, g, 0)),
            seq(gw, 0), seq(D_STATE, b_block0),
            pl.BlockSpec((1, D_STATE, s), lambda i, g, j: (i, g, 0), pipeline_mode=once),
            seq(D_STATE, c_block0),
            seq(LANES, 0),
            pl.BlockSpec((1, 1, DT_ROWS, s), lambda i, g, j: (i, g, 0, 0)),
            gspec(1), gspec(1), gspec(DT_ROWS), gspec(DT_ROWS),
            pl.BlockSpec((1, gw), lambda i, g, j: (0, g)),
            _const_spec((CHUNK, CHUNK)), _const_spec((CHUNK, CHUNK)),
        ],
        out_specs=[
            pl.BlockSpec((1, tq, aw), lambda i, g, j: (i, j, g)),
            pl.BlockSpec((1, s, gw), lambda i, g, j: (i, 0, g)),
        ],
        out_shape=[
            jax.ShapeDtypeStruct((b, s, ATTN_WIDTH), BF16),
            jax.ShapeDtypeStruct((b, s, D_INNER), BF16),
        ],
        scratch_shapes=[
            pltpu.VMEM((KV_WIDTH, REP * tq), BF16),
            pltpu.VMEM((1, REP * tq), F32),
            pltpu.VMEM((2, min(KEY_CHUNK, s), REP * tq), BF16),
            pltpu.VMEM((s, gw), F32),
            pltpu.VMEM((D_STATE, gw), F32),
            pltpu.VMEM((D_STATE, gw), F32),
        ],
        compiler_params=_params(("parallel", "arbitrary", "arbitrary")),
        name="mixer",
    )(score_bound, qt, kn, vt, xc, xc, bt, xc, dt_pad, dt_t, dtb, alog, dtb_t, alog_t, dskip, tril, triu)


def _outmlp_kernel(attn_ref, y_ref, z_ref, x_ref, nw_ref, woa_ref, wos_ref, ln2_ref, wup_ref, wdn_ref,
                   fin_ref, o_ref, *, final_norm):
    gated = []
    for gi in range(N_SSM_GROUPS):
        cols = slice(gi * GROUP_WIDTH, (gi + 1) * GROUP_WIDTH)
        yg = y_ref[0, :, cols].astype(F32) * _silu(z_ref[0, :, cols].astype(F32))
        msg = jnp.mean(yg * yg, axis=-1, keepdims=True)
        gated.append((yg * lax.rsqrt(msg + NORM_EPS) * nw_ref[:, cols]).astype(BF16))
    ssm = jnp.concatenate(gated, axis=1)
    x1 = x_ref[0] + _dot(attn_ref[0], woa_ref[...]) + _dot(ssm, wos_ref[...])
    ms = jnp.mean(x1 * x1, axis=-1, keepdims=True)
    h = (x1 * lax.rsqrt(ms + NORM_EPS) * ln2_ref[...]).astype(BF16)
    u = jnp.maximum(_dot(h, wup_ref[...]), 0.0)
    x2 = x1 + _dot((u * u).astype(BF16), wdn_ref[...])
    if final_norm:
        ms2 = jnp.mean(x2 * x2, axis=-1, keepdims=True)
        x2 = x2 * lax.rsqrt(ms2 + NORM_EPS) * fin_ref[...]
    o_ref[0] = x2


def _out_mlp(attn, y, z, x, norm_w, wo_attn, wo_ssm, ln2, w_up, w_down, fin_w, final_norm):
    b, s, _ = x.shape
    tm = min(TM_MLP, s)
    grid = (b, s // tm)
    tok = lambda width: pl.BlockSpec((1, tm, width), lambda i, j: (i, j, 0))
    return pl.pallas_call(
        functools.partial(_outmlp_kernel, final_norm=final_norm),
        grid=grid,
        in_specs=[
            tok(ATTN_WIDTH), tok(D_INNER), tok(D_INNER), tok(D_MODEL),
            _const_spec((1, D_INNER)),
            _const_spec((ATTN_WIDTH, D_MODEL)), _const_spec((D_INNER, D_MODEL)),
            _const_spec((1, D_MODEL)),
            _const_spec((D_MODEL, D_FF)), _const_spec((D_FF, D_MODEL)),
            _const_spec((1, D_MODEL)),
        ],
        out_specs=tok(D_MODEL),
        out_shape=jax.ShapeDtypeStruct((b, s, D_MODEL), F32),
        compiler_params=_params(("parallel", "parallel")),
        name="out_mlp",
    )(attn, y, z, x, norm_w, wo_attn, wo_ssm, ln2, w_up, w_down, fin_w)


def _rope_tables(seq):
    rows = seq // GRID_W
    row_ids = jnp.repeat(jnp.arange(rows, dtype=jnp.int32), GRID_W)
    col_ids = jnp.tile(jnp.arange(GRID_W, dtype=jnp.int32), rows)
    half = HEAD_DIM // 2
    inv_freq = ROPE_THETA ** (-jnp.arange(0, half, 2, dtype=F32) / half)

    def ang(pos):
        a = pos.astype(F32)[:, None] * inv_freq[None, :]
        return jnp.concatenate([a, a], axis=-1)

    a = jnp.concatenate([ang(row_ids), ang(col_ids)], axis=-1)
    sign = jnp.tile(jnp.concatenate([-jnp.ones((half // 2,), F32), jnp.ones((half // 2,), F32)]), 2)
    cos = jnp.cos(a)
    sin = jnp.sin(a) * sign[None, :]
    return jnp.tile(cos, (1, LANES // HEAD_DIM)), jnp.tile(sin, (1, LANES // HEAD_DIM))


def _constants():
    seg = np.kron(np.eye(N_Q_HEADS, dtype=np.float32), np.ones((HEAD_DIM, HEAD_DIM), np.float32))
    tril = np.tril(np.ones((CHUNK, CHUNK), np.float32))
    return dict(
        seg=jnp.asarray(seg, BF16),
        tril=jnp.asarray(tril, BF16), triu=jnp.asarray(tril.T, BF16),
    )


def _group_rows(vf, vb):
    return jnp.concatenate([vf.reshape(N_SSM_GROUPS, HEADS_PER_GROUP),
                            vb.reshape(N_SSM_GROUPS, HEADS_PER_GROUP)], axis=1)


def _layer_weights(w_in, dt_bias_f, dt_bias_b, a_log_f, a_log_b):
    wdt = w_in[:, XBC_OFF + CONV_DIM:]
    wf = wdt[:, :N_SSM_HEADS].reshape(D_MODEL, N_SSM_GROUPS, HEADS_PER_GROUP)
    wb = wdt[:, N_SSM_HEADS:].reshape(D_MODEL, N_SSM_GROUPS, HEADS_PER_GROUP)
    grp = jnp.concatenate([wf, wb], axis=-1)
    wdt_pad = jnp.pad(grp, ((0, 0), (0, 0), (0, LANES - DT_ROWS))).reshape(D_MODEL, DT_PAD)
    w_all = jnp.concatenate([w_in[:, :DT_OFF], wdt_pad], axis=1).astype(BF16)
    wdt_t = grp.reshape(D_MODEL, N_SSM_GROUPS * DT_ROWS).T.astype(BF16)
    pad_lanes = lambda v: jnp.pad(v, ((0, 0), (0, LANES - DT_ROWS)))[:, None, :]
    dtb = _group_rows(dt_bias_f, dt_bias_b)
    alog = _group_rows(a_log_f, a_log_b)
    bcast = lambda v: jnp.broadcast_to(v[:, :, None], (N_SSM_GROUPS, DT_ROWS, LANES))
    return w_all, wdt_t, pad_lanes(dtb), pad_lanes(alog), bcast(dtb), bcast(alog)


def kernel(x, ln1_w, w_in, conv_w, conv_b, dt_bias_fwd, dt_bias_bwd, a_log_fwd, a_log_bwd, d_skip,
           ssm_norm_w, q_norm_w, k_norm_w, w_out, ln2_w, w_up, w_down, final_norm_w):
    b, s, _ = x.shape
    depth = w_in.shape[0]
    consts = _constants()
    cos_t, sin_t = _rope_tables(s)
    row = lambda v: v.reshape(1, -1).astype(F32)
    for i in range(depth):
        w_all, wdt_t, dtb, alog, dtb_t, alog_t = _layer_weights(
            w_in[i], dt_bias_fwd[i], dt_bias_bwd[i], a_log_fwd[i], a_log_bwd[i])
        qkv, z, xc, bt, dt_pad, dt_t = _in_proj(
            x, row(ln1_w[i]), w_all, wdt_t, conv_w[i].astype(F32), row(conv_b[i]))
        qt, kn, vt = _qk_prep(
            qkv, cos_t, sin_t, row(jnp.tile(q_norm_w[i], N_Q_HEADS)), row(jnp.tile(k_norm_w[i], N_KV_HEADS)),
            consts["seg"])
        score_bound = (HEAD_DIM ** 0.5 * LOG2E * jnp.max(jnp.abs(q_norm_w[i])) * jnp.max(jnp.abs(k_norm_w[i]))
                       ).astype(F32).reshape(1, 1)
        attn, y = _mixer(
            score_bound, qt, kn, vt, xc, bt, dt_pad, dt_t.reshape(b, N_SSM_GROUPS, DT_ROWS, s), dtb, alog, dtb_t, alog_t,
            row(jnp.repeat(d_skip[i], SSM_HEAD_DIM)), consts["tril"], consts["triu"])
        x = _out_mlp(
            attn, y, z, x, row(ssm_norm_w[i]),
            w_out[i, :ATTN_WIDTH].astype(BF16), w_out[i, ATTN_WIDTH:].astype(BF16),
            row(ln2_w[i]), w_up[i].astype(BF16), w_down[i].astype(BF16), row(final_norm_w),
            final_norm=(i == depth - 1))
    return x
```

```python
import functools

import numpy as np
import jax
import jax.numpy as jnp
from jax import lax
from jax.experimental import pallas as pl
from jax.experimental.pallas import tpu as pltpu

F32 = jnp.float32
BF16 = jnp.bfloat16

D_MODEL = 1024
GRID_W = 64
N_Q_HEADS = 16
N_KV_HEADS = 4
HEAD_DIM = 64
REP = N_Q_HEADS // N_KV_HEADS
ATTN_WIDTH = N_Q_HEADS * HEAD_DIM
KV_WIDTH = N_KV_HEADS * HEAD_DIM
ROPE_THETA = 10000.0
D_INNER = 2048
SSM_HEAD_DIM = 64
N_SSM_HEADS = D_INNER // SSM_HEAD_DIM
N_SSM_GROUPS = 4
HEADS_PER_GROUP = N_SSM_HEADS // N_SSM_GROUPS
GROUP_WIDTH = D_INNER // N_SSM_GROUPS
D_STATE = 128
D_CONV = 5
CHUNK = 128
CONV_DIM = D_INNER + 2 * N_SSM_GROUPS * D_STATE
D_FF = 4 * D_MODEL
NORM_EPS = 1e-5
QK_EPS = 1e-6
LOG2E = 1.4426950408889634

LANES = 128
VMEM_LIMIT_BYTES = 58 * 1024 * 1024

QKV_WIDTH = ATTN_WIDTH + 2 * KV_WIDTH
Z_OFF = QKV_WIDTH
XBC_OFF = Z_OFF + D_INNER
DT_OFF = XBC_OFF + CONV_DIM
DT_PAD = N_SSM_GROUPS * LANES
W_ALL_WIDTH = DT_OFF + DT_PAD
DT_ROWS = 2 * HEADS_PER_GROUP

TM_PROJ = 512
TS_PREP = 512
TQ = 512
KEY_CHUNK = 512
PV_CHUNKS = 2
VT_ROWS = HEAD_DIM + 16
SAFE_SHIFT = 60.0
TM_MLP = 512
COPY_ROWS = 256
HALO = 16
CONV_STRIP = 256
PLAIN_STRIP = 512
CONV_ROWS = 128


def _dot(a, b):
    return jnp.dot(a, b, preferred_element_type=F32)


def _dot_nt(a, b):
    return lax.dot_general(a, b, (((1,), (1,)), ((), ())), preferred_element_type=F32)


def _params(semantics, flags=None):
    return pltpu.CompilerParams(dimension_semantics=semantics, vmem_limit_bytes=VMEM_LIMIT_BYTES, flags=flags)


def _const_spec(shape):
    nd = len(shape)
    return pl.BlockSpec(shape, lambda *_: (0,) * nd, pipeline_mode=pl.Buffered(1))


def _silu(v):
    return v * (1.0 / (1.0 + jnp.exp(-v)))


def _inproj_kernel(x_ref, xp_ref, xn_ref, ln_ref, w_ref, wdt_t_ref, cw_ref, cb_ref,
                   qkv_ref, z_ref, xc_ref, bt_ref, dt_ref, dtt_ref, win_scr):
    j = pl.program_id(1)
    tm = x_ref.shape[1]

    def normed(x):
        ms = jnp.mean(x * x, axis=-1, keepdims=True)
        return (x * lax.rsqrt(ms + NORM_EPS) * ln_ref[...]).astype(BF16)

    hn = normed(x_ref[0])
    dtt_ref[0] = _dot_nt(wdt_t_ref[...], hn)

    def plain_job(out_ref, w_off, o_off):
        def run():
            res = _dot(hn, w_ref[:, w_off:w_off + PLAIN_STRIP])
            out_ref[0, :, o_off:o_off + PLAIN_STRIP] = res.astype(out_ref.dtype)
        return run

    plain_jobs = (
        [plain_job(qkv_ref, o, o) for o in range(0, QKV_WIDTH, PLAIN_STRIP)]
        + [plain_job(z_ref, Z_OFF + o, o) for o in range(0, D_INNER, PLAIN_STRIP)]
        + [plain_job(dt_ref, DT_OFF + o, o) for o in range(0, DT_PAD, PLAIN_STRIP)])

    hn_ext = jnp.concatenate([normed(xp_ref[0]), hn, normed(xn_ref[0])], axis=0)
    ext_row = lax.broadcasted_iota(jnp.int32, (tm + 2 * HALO, 1), 0)
    outside = ((ext_row < HALO) & (j == 0)) | ((ext_row >= HALO + tm) & (j == pl.num_programs(1) - 1))
    pad = D_CONV // 2
    n_strips = CONV_DIM // CONV_STRIP
    project = lambda t: _dot(hn_ext, w_ref[:, XBC_OFF + t * CONV_STRIP:XBC_OFF + (t + 1) * CONV_STRIP])
    nxt = project(0)
    for t in range(n_strips):
        cur = nxt
        if t + 1 < n_strips:
            nxt = project(t + 1)
        if plain_jobs:
            plain_jobs.pop(0)()
        win = win_scr.at[t % 2]
        win[...] = jnp.where(outside, 0.0, cur)
        cols = slice(t * CONV_STRIP, (t + 1) * CONV_STRIP)
        b_lo = D_INNER // CONV_STRIP
        b_hi = b_lo + N_SSM_GROUPS * D_STATE // CONV_STRIP
        for r0 in range(0, tm, CONV_ROWS):
            acc = win[HALO - pad + r0:HALO - pad + r0 + CONV_ROWS, :] * cw_ref[0:1, cols] + cb_ref[:, cols]
            for k in range(1, D_CONV):
                off = HALO - pad + k + r0
                acc = acc + win[off:off + CONV_ROWS, :] * cw_ref[k:k + 1, cols]
            y = _silu(acc)
            xc_ref[0, r0:r0 + CONV_ROWS, cols] = y.astype(BF16)
            if b_lo <= t < b_hi:
                bt_ref[0, (t - b_lo) * CONV_STRIP:(t - b_lo + 1) * CONV_STRIP, r0:r0 + CONV_ROWS] = (
                    y.T.astype(BF16))
    for job in plain_jobs:
        job()


def _in_proj(x, ln_w, w_all, wdt_t, conv_w, conv_b):
    b, s, _ = x.shape
    tm = min(TM_PROJ, s)
    grid = (b, s // tm)
    halo_blocks = tm // HALO
    last_halo_block = s // HALO - 1
    tok = lambda width: pl.BlockSpec((1, tm, width), lambda i, j: (i, j, 0))
    bn = N_SSM_GROUPS * D_STATE
    return pl.pallas_call(
        _inproj_kernel,
        grid=grid,
        in_specs=[
            tok(D_MODEL),
            pl.BlockSpec((1, HALO, D_MODEL), lambda i, j: (i, jnp.maximum(j * halo_blocks - 1, 0), 0)),
            pl.BlockSpec((1, HALO, D_MODEL),
                         lambda i, j: (i, jnp.minimum((j + 1) * halo_blocks, last_halo_block), 0)),
            _const_spec((1, D_MODEL)),
            _const_spec((D_MODEL, W_ALL_WIDTH)),
            _const_spec((N_SSM_GROUPS * DT_ROWS, D_MODEL)),
            _const_spec((D_CONV, CONV_DIM)),
            _const_spec((1, CONV_DIM)),
        ],
        out_specs=[
            tok(QKV_WIDTH), tok(D_INNER), tok(CONV_DIM),
            pl.BlockSpec((1, bn, tm), lambda i, j: (i, 0, j)),
            tok(DT_PAD),
            pl.BlockSpec((1, N_SSM_GROUPS * DT_ROWS, tm), lambda i, j: (i, 0, j)),
        ],
        out_shape=[
            jax.ShapeDtypeStruct((b, s, QKV_WIDTH), BF16),
            jax.ShapeDtypeStruct((b, s, D_INNER), BF16),
            jax.ShapeDtypeStruct((b, s, CONV_DIM), BF16),
            jax.ShapeDtypeStruct((b, bn, s), BF16),
            jax.ShapeDtypeStruct((b, s, DT_PAD), F32),
            jax.ShapeDtypeStruct((b, N_SSM_GROUPS * DT_ROWS, s), F32),
        ],
        scratch_shapes=[pltpu.VMEM((2, tm + 2 * HALO, CONV_STRIP), F32)],
        compiler_params=_params(("parallel", "parallel")),
        name="in_proj",
    )(x, x, x, ln_w, w_all, wdt_t, conv_w, conv_b)


def _norm_rope(t, w, seg_ones, cos, sin_signed, post_scale):
    width = t.shape[1]
    ss = _dot((t * t).astype(BF16), seg_ones) * (1.0 / HEAD_DIM)
    y = t * lax.rsqrt(ss + QK_EPS) * w
    reps = width // LANES
    cosw = jnp.concatenate([cos] * reps, axis=1)
    sinw = jnp.concatenate([sin_signed] * reps, axis=1)
    quarter = HEAD_DIM // 4
    upper = pltpu.roll(y, width - quarter, 1)
    lower = pltpu.roll(y, quarter, 1)
    lane = lax.broadcasted_iota(jnp.int32, y.shape, 1)
    first_half = (lane % (2 * quarter)) < quarter
    rot = jnp.where(first_half, upper, lower)
    return (y * cosw + rot * sinw) * post_scale


def _qkprep_kernel(q_ref, k_ref, v_ref, cos_ref, sin_ref, qw_ref, kw_ref, seg_ref,
                   qt_ref, kn_ref, vt_ref):
    cos = cos_ref[...]
    sin = sin_ref[...]
    q = _norm_rope(q_ref[0].astype(F32), qw_ref[...], seg_ref[...], cos, sin,
                   (HEAD_DIM ** -0.5) * LOG2E)
    qt_ref[0] = q.T.astype(BF16)
    k = _norm_rope(k_ref[0].astype(F32), kw_ref[...], seg_ref[0:KV_WIDTH, 0:KV_WIDTH], cos, sin, 1.0)
    kn_ref[0] = k.astype(BF16)
    vt = v_ref[0].astype(F32).T.astype(BF16)
    ones = jnp.ones((VT_ROWS - HEAD_DIM, vt.shape[1]), BF16)
    vt_ref[0] = jnp.concatenate(
        [piece for g in range(N_KV_HEADS) for piece in (vt[g * HEAD_DIM:(g + 1) * HEAD_DIM], ones)], axis=0)


def _qk_prep(qkv, cos_t, sin_t, qw, kw, seg_ones):
    b, s, _ = qkv.shape
    ts = min(TS_PREP, s)
    grid = (b, s // ts)
    return pl.pallas_call(
        _qkprep_kernel,
        grid=grid,
        in_specs=[
            pl.BlockSpec((1, ts, ATTN_WIDTH), lambda i, j: (i, j, 0)),
            pl.BlockSpec((1, ts, KV_WIDTH), lambda i, j: (i, j, ATTN_WIDTH // KV_WIDTH)),
            pl.BlockSpec((1, ts, KV_WIDTH), lambda i, j: (i, j, ATTN_WIDTH // KV_WIDTH + 1)),
            pl.BlockSpec((ts, LANES), lambda i, j: (j, 0)),
            pl.BlockSpec((ts, LANES), lambda i, j: (j, 0)),
            _const_spec((1, ATTN_WIDTH)),
            _const_spec((1, KV_WIDTH)),
            _const_spec((ATTN_WIDTH, ATTN_WIDTH)),
        ],
        out_specs=[
            pl.BlockSpec((1, ATTN_WIDTH, ts), lambda i, j: (i, 0, j)),
            pl.BlockSpec((1, ts, KV_WIDTH), lambda i, j: (i, j, 0)),
            pl.BlockSpec((1, N_KV_HEADS * VT_ROWS, ts), lambda i, j: (i, 0, j)),
        ],
        out_shape=[
            jax.ShapeDtypeStruct((b, ATTN_WIDTH, s), BF16),
            jax.ShapeDtypeStruct((b, s, KV_WIDTH), BF16),
            jax.ShapeDtypeStruct((b, N_KV_HEADS * VT_ROWS, s), BF16),
        ],
        compiler_params=_params(("parallel", "parallel")),
        name="qk_prep",
    )(qkv, qkv, qkv, cos_t, sin_t, qw, kw, seg_ones)


def _attn_setup(qt_ref, k_ref, bound_ref, qe_scr, shift_scr):
    g = pl.program_id(1)
    s_len = k_ref.shape[1]
    kc = min(KEY_CHUNK, s_len)
    qt = qt_ref[0]
    q_cols = jnp.concatenate([qt[r * HEAD_DIM:(r + 1) * HEAD_DIM, :] for r in range(REP)], axis=1)
    qe_scr[...] = jnp.zeros_like(qe_scr)
    qe_scr[pl.ds(pl.multiple_of(g * HEAD_DIM, HEAD_DIM), HEAD_DIM), :] = q_cols
    bound = bound_ref[0, 0]

    @pl.when(bound <= SAFE_SHIFT)
    def _():
        shift_scr[...] = jnp.full(shift_scr.shape, bound, F32)

    @pl.when(bound > SAFE_SHIFT)
    def _():
        def body(c, mx):
            r0 = pl.multiple_of(c * kc, kc)
            st = _dot(k_ref[0, pl.ds(r0, kc), :], qe_scr[...])
            return jnp.maximum(mx, jnp.max(st, axis=0, keepdims=True))

        shift_scr[...] = lax.fori_loop(0, s_len // kc, body, jnp.full(shift_scr.shape, -jnp.inf, F32))


def _attn_stages(k_ref, vt_ref, o_ref, qe_scr, shift_scr, p_scr):
    tq = o_ref.shape[1]
    s_len = k_ref.shape[1]
    kc = min(KEY_CHUNK, s_len)
    n_kc = s_len // kc
    n_cols = REP * tq
    acc = jnp.zeros((VT_ROWS, n_cols), F32)
    group = min(PV_CHUNKS, n_kc)
    scores = lambda c: _dot(k_ref[0, c * kc:(c + 1) * kc, :], qe_scr[...])
    st_next = scores(0)
    for c in range(n_kc + 1):
        st = st_next
        if c + 1 < n_kc:
            st_next = scores(c + 1)
        if c > 0 and c % group == 0:
            first = c - group
            acc = acc + _dot(vt_ref[0, :, first * kc:c * kc], p_scr[(first // group) % 2])
        yield
        if c < n_kc:
            p_scr[(c // group) % 2, (c % group) * kc:(c % group + 1) * kc, :] = (
                jnp.exp2(st - shift_scr[...]).astype(BF16))
            yield
    ot = (acc[:HEAD_DIM] / acc[HEAD_DIM:HEAD_DIM + 1]).T
    out = jnp.concatenate([ot[r * tq:(r + 1) * tq, :] for r in range(REP)], axis=1)
    o_ref[0] = out.astype(o_ref.dtype)


def _split_hi_lo(v):
    hi = v.astype(BF16)
    lo = (v - hi.astype(F32)).astype(BF16)
    return hi, lo


def _softplus(v):
    return jnp.maximum(v, 0.0) + jnp.log1p(jnp.exp(-jnp.abs(v)))


def _mixer_kernel(bound_ref, qt_ref, k_ref, vt_ref,
                  xc_ref, b_ref, bt_ref, c_ref, dt_ref, dtt_ref,
                  dtb_ref, alog_ref, dtbt_ref, alogt_ref, dskip_ref,
                  tril_ref, triu_ref,
                  attn_ref, y_ref,
                  qe_scr, shift_scr, p_scr, y_scr, stf_scr, stb_scr):
    _attn_setup(qt_ref, k_ref, bound_ref, qe_scr, shift_scr)
    j = pl.program_id(2)
    n_blocks = pl.num_programs(2)
    s_len = xc_ref.shape[1]
    n_chunks = s_len // CHUNK
    iters = n_chunks // (s_len // qt_ref.shape[2])
    hp = HEADS_PER_GROUP
    rows = min(COPY_ROWS, s_len)

    @pl.when(j == 0)
    def _():
        def skip_body(i, carry):
            r0 = pl.multiple_of(i * rows, rows)
            y_scr[pl.ds(r0, rows), :] = dskip_ref[...] * xc_ref[0, pl.ds(r0, rows), :].astype(F32)
            return carry

        lax.fori_loop(0, s_len // rows, skip_body, 0)
        stf_scr[...] = jnp.zeros_like(stf_scr)
        stb_scr[...] = jnp.zeros_like(stb_scr)

    row = lax.broadcasted_iota(jnp.int32, (CHUNK, CHUNK), 0)
    col = lax.broadcasted_iota(jnp.int32, (CHUNK, CHUNK), 1)
    lane = lax.broadcasted_iota(jnp.int32, (CHUNK, LANES), 1)
    lane_lt_half = lane < SSM_HEAD_DIM
    neg_a = -jnp.exp(alog_ref[0]) * LOG2E
    neg_a_t = -jnp.exp(alogt_ref[0]) * LOG2E

    def chunk_step(forward, c, st_scr):
        tri = tril_ref[...] if forward else triu_ref[...]
        tri_t = triu_ref[...] if forward else tril_ref[...]
        keep = (row >= col) if forward else (row <= col)
        lane0 = 0 if forward else hp
        last = CHUNK - 1 if forward else 0
        r0 = pl.multiple_of(c * CHUNK, CHUNK)
        bc = b_ref[0, pl.ds(r0, CHUNK), :]
        bt = bt_ref[0, :, pl.ds(r0, CHUNK)]
        cc = c_ref[0, pl.ds(r0, CHUNK), :]
        dts = _softplus(dt_ref[0, pl.ds(r0, CHUNK), :] + dtb_ref[0])
        a_hi, a_lo = _split_hi_lo(dts * neg_a)
        a_cat = jnp.concatenate([a_hi, a_lo], axis=1)
        dts_t = _softplus(dtt_ref[0, 0, :, pl.ds(r0, CHUNK)] + dtbt_ref[0])
        at_hi, at_lo = _split_hi_lo(dts_t * neg_a_t)
        at_cat = jnp.concatenate([at_hi, at_lo], axis=0)
        yield
        acs2 = _dot(tri, a_cat)
        acst2 = _dot(at_cat, tri_t)
        cb = _dot_nt(cc, bc)
        yield
        acs = acs2[:, :LANES] + acs2[:, LANES:]
        acs_t = acst2[:DT_ROWS] + acst2[DT_ROWS:]
        ms, dt_tiles, acs_tiles = [], [], []
        for j in range(hp // 2):
            pair, dt_cols, acs_cols = [], [], []
            for h in (2 * j, 2 * j + 1):
                acs_col = jnp.broadcast_to(acs[:, lane0 + h:lane0 + h + 1], (CHUNK, LANES))
                dt_cols.append(jnp.broadcast_to(dts[:, lane0 + h:lane0 + h + 1], (CHUNK, LANES)))
                acs_cols.append(acs_col)
                seg = acs_col - acs_t[lane0 + h:lane0 + h + 1, :]
                lmat = jnp.exp2(jnp.where(keep, seg, -jnp.inf))
                pair.append((cb * lmat).astype(BF16))
            ms.append(jnp.concatenate(pair, axis=1))
            dt_tiles.append(jnp.where(lane_lt_half, dt_cols[0], dt_cols[1]))
            acs_tiles.append(jnp.where(lane_lt_half, acs_cols[0], acs_cols[1]))
        dt_exp = jnp.concatenate(dt_tiles, axis=1)
        acs_exp = jnp.concatenate(acs_tiles, axis=1)
        acs_last = acs_exp[last:last + 1, :]
        xdt = xc_ref[0, pl.ds(r0, CHUNK), :].astype(F32) * dt_exp
        xdt_b = xdt.astype(BF16)
        xd = (xdt * jnp.exp2(acs_last - acs_exp)).astype(BF16)
        rhs = []
        for j in range(hp // 2):
            xp = xdt_b[:, j * LANES:(j + 1) * LANES]
            zero = jnp.zeros_like(xp)
            rhs.append(jnp.concatenate([jnp.where(lane_lt_half, xp, zero),
                                        jnp.where(lane_lt_half, zero, xp)], axis=0))
        yield
        state = st_scr[...]
        y_off = _dot(cc, state.astype(BF16))
        st_new = _dot(bt, xd)
        pairs = [_dot(ms[j], rhs[j]) for j in range(hp // 2)]
        yield
        st_scr[...] = state * jnp.exp2(acs_last) + st_new
        y_scr[pl.ds(r0, CHUNK), :] += y_off * jnp.exp2(acs_exp) + jnp.concatenate(pairs, axis=1)

    def scan_stages():
        for it in range(iters):
            i = j * iters + it
            active = [chunk_step(True, i, stf_scr), chunk_step(False, n_chunks - 1 - i, stb_scr)]
            while active:
                active = [gen for gen in active if next(gen, True) is None]
                yield

    streams = [_attn_stages(k_ref, vt_ref, attn_ref, qe_scr, shift_scr, p_scr), scan_stages()]
    while streams:
        streams = [gen for gen in streams if next(gen, True) is None]

    @pl.when(j == n_blocks - 1)
    def _():
        def out_body(i, carry):
            r0 = pl.multiple_of(i * rows, rows)
            y_ref[0, pl.ds(r0, rows), :] = y_scr[pl.ds(r0, rows), :].astype(y_ref.dtype)
            return carry

        lax.fori_loop(0, s_len // rows, out_body, 0)


def _mixer(score_bound, qt, kn, vt, xc, bt, dt_pad, dt_t, dtb, alog, dtb_t, alog_t, dskip, tril, triu):
    b, s, _ = xc.shape
    tq = min(TQ, s)
    assert (s // CHUNK) % (s // tq) == 0
    grid = (b, N_SSM_GROUPS, s // tq)
    gw = GROUP_WIDTH
    aw = REP * HEAD_DIM
    b_block0 = D_INNER // D_STATE
    c_block0 = b_block0 + N_SSM_GROUPS
    once = pl.Buffered(1)
    seq = lambda width, off: pl.BlockSpec((1, s, width), lambda i, g, j: (i, 0, off + g), pipeline_mode=once)
    gspec = lambda rows: pl.BlockSpec((1, rows, LANES), lambda i, g, j: (g, 0, 0))
    return pl.pallas_call(
        _mixer_kernel,
        grid=grid,
        in_specs=[
            pl.BlockSpec(memory_space=pltpu.SMEM),
            pl.BlockSpec((1, aw, tq), lambda i, g, j: (i, g, j)),
            pl.BlockSpec((1, s, KV_WIDTH), lambda i, g, j: (i, 0, 0)),
            pl.BlockSpec((1, VT_ROWS, s), lambda i, g, j: (i, g, 0)),
            seq(gw, 0), seq(D_STATE, b_block0),
            pl.BlockSpec((1, D_STATE, s), lambda i, g, j: (i, g, 0), pipeline_mode=once),
            seq(D_STATE, c_block0),
            seq(LANES, 0),
            pl.BlockSpec((1, 1, DT_ROWS, s), lambda i, g, j: (i, g, 0, 0)),
            gspec(1), gspec(1), gspec(DT_ROWS), gspec(DT_ROWS),
            pl.BlockSpec((1, gw), lambda i, g, j: (0, g)),
            _const_spec((CHUNK, CHUNK)), _const_spec((CHUNK, CHUNK)),
        ],
        out_specs=[
            pl.BlockSpec((1, tq, aw), lambda i, g, j: (i, j, g)),
            pl.BlockSpec((1, s, gw), lambda i, g, j: (i, 0, g), pipeline_mode=once),
        ],
        out_shape=[
            jax.ShapeDtypeStruct((b, s, ATTN_WIDTH), BF16),
            jax.ShapeDtypeStruct((b, s, D_INNER), BF16),
        ],
        scratch_shapes=[
            pltpu.VMEM((KV_WIDTH, REP * tq), BF16),
            pltpu.VMEM((1, REP * tq), F32),
            pltpu.VMEM((2, min(PV_CHUNKS * KEY_CHUNK, s), REP * tq), BF16),
            pltpu.VMEM((s, gw), F32),
            pltpu.VMEM((D_STATE, gw), F32),
            pltpu.VMEM((D_STATE, gw), F32),
        ],
        compiler_params=_params(("parallel", "arbitrary", "arbitrary")),
        name="mixer",
    )(score_bound, qt, kn, vt, xc, xc, bt, xc, dt_pad, dt_t, dtb, alog, dtb_t, alog_t, dskip, tril, triu)


def _outmlp_kernel(attn_ref, y_ref, z_ref, x_ref, nw_ref, woa_ref, wos_ref, ln2_ref, wup_ref, wdn_ref,
                   fin_ref, o_ref, *, final_norm):
    gated = []
    for gi in range(N_SSM_GROUPS):
        cols = slice(gi * GROUP_WIDTH, (gi + 1) * GROUP_WIDTH)
        yg = y_ref[0, :, cols].astype(F32) * _silu(z_ref[0, :, cols].astype(F32))
        msg = jnp.mean(yg * yg, axis=-1, keepdims=True)
        gated.append((yg * lax.rsqrt(msg + NORM_EPS) * nw_ref[:, cols]).astype(BF16))
    ssm = jnp.concatenate(gated, axis=1)
    x1 = x_ref[0] + _dot(attn_ref[0], woa_ref[...]) + _dot(ssm, wos_ref[...])
    ms = jnp.mean(x1 * x1, axis=-1, keepdims=True)
    h = (x1 * lax.rsqrt(ms + NORM_EPS) * ln2_ref[...]).astype(BF16)
    u = jnp.maximum(_dot(h, wup_ref[...]), 0.0)
    x2 = x1 + _dot((u * u).astype(BF16), wdn_ref[...])
    if final_norm:
        ms2 = jnp.mean(x2 * x2, axis=-1, keepdims=True)
        x2 = x2 * lax.rsqrt(ms2 + NORM_EPS) * fin_ref[...]
    o_ref[0] = x2


def _out_mlp(attn, y, z, x, norm_w, wo_attn, wo_ssm, ln2, w_up, w_down, fin_w, final_norm):
    b, s, _ = x.shape
    tm = min(TM_MLP, s)
    grid = (b, s // tm)
    tok = lambda width: pl.BlockSpec((1, tm, width), lambda i, j: (i, j, 0))
    return pl.pallas_call(
        functools.partial(_outmlp_kernel, final_norm=final_norm),
        grid=grid,
        in_specs=[
            tok(ATTN_WIDTH), tok(D_INNER), tok(D_INNER), tok(D_MODEL),
            _const_spec((1, D_INNER)),
            _const_spec((ATTN_WIDTH, D_MODEL)), _const_spec((D_INNER, D_MODEL)),
            _const_spec((1, D_MODEL)),
            _const_spec((D_MODEL, D_FF)), _const_spec((D_FF, D_MODEL)),
            _const_spec((1, D_MODEL)),
        ],
        out_specs=tok(D_MODEL),
        out_shape=jax.ShapeDtypeStruct((b, s, D_MODEL), F32),
        compiler_params=_params(("parallel", "parallel")),
        name="out_mlp",
    )(attn, y, z, x, norm_w, wo_attn, wo_ssm, ln2, w_up, w_down, fin_w)


def _rope_tables(seq):
    rows = seq // GRID_W
    row_ids = jnp.repeat(jnp.arange(rows, dtype=jnp.int32), GRID_W)
    col_ids = jnp.tile(jnp.arange(GRID_W, dtype=jnp.int32), rows)
    half = HEAD_DIM // 2
    inv_freq = ROPE_THETA ** (-jnp.arange(0, half, 2, dtype=F32) / half)

    def ang(pos):
        a = pos.astype(F32)[:, None] * inv_freq[None, :]
        return jnp.concatenate([a, a], axis=-1)

    a = jnp.concatenate([ang(row_ids), ang(col_ids)], axis=-1)
    sign = jnp.tile(jnp.concatenate([-jnp.ones((half // 2,), F32), jnp.ones((half // 2,), F32)]), 2)
    cos = jnp.cos(a)
    sin = jnp.sin(a) * sign[None, :]
    return jnp.tile(cos, (1, LANES // HEAD_DIM)), jnp.tile(sin, (1, LANES // HEAD_DIM))


def _constants():
    seg = np.kron(np.eye(N_Q_HEADS, dtype=np.float32), np.ones((HEAD_DIM, HEAD_DIM), np.float32))
    tril = np.tril(np.ones((CHUNK, CHUNK), np.float32))
    return dict(
        seg=jnp.asarray(seg, BF16),
        tril=jnp.asarray(tril, BF16), triu=jnp.asarray(tril.T, BF16),
    )


def _group_rows(vf, vb):
    return jnp.concatenate([vf.reshape(N_SSM_GROUPS, HEADS_PER_GROUP),
                            vb.reshape(N_SSM_GROUPS, HEADS_PER_GROUP)], axis=1)


def _layer_weights(w_in, dt_bias_f, dt_bias_b, a_log_f, a_log_b):
    wdt = w_in[:, XBC_OFF + CONV_DIM:]
    wf = wdt[:, :N_SSM_HEADS].reshape(D_MODEL, N_SSM_GROUPS, HEADS_PER_GROUP)
    wb = wdt[:, N_SSM_HEADS:].reshape(D_MODEL, N_SSM_GROUPS, HEADS_PER_GROUP)
    grp = jnp.concatenate([wf, wb], axis=-1)
    wdt_pad = jnp.pad(grp, ((0, 0), (0, 0), (0, LANES - DT_ROWS))).reshape(D_MODEL, DT_PAD)
    w_all = jnp.concatenate([w_in[:, :DT_OFF], wdt_pad], axis=1).astype(BF16)
    wdt_t = grp.reshape(D_MODEL, N_SSM_GROUPS * DT_ROWS).T.astype(BF16)
    pad_lanes = lambda v: jnp.pad(v, ((0, 0), (0, LANES - DT_ROWS)))[:, None, :]
    dtb = _group_rows(dt_bias_f, dt_bias_b)
    alog = _group_rows(a_log_f, a_log_b)
    bcast = lambda v: jnp.broadcast_to(v[:, :, None], (N_SSM_GROUPS, DT_ROWS, LANES))
    return w_all, wdt_t, pad_lanes(dtb), pad_lanes(alog), bcast(dtb), bcast(alog)


def kernel(x, ln1_w, w_in, conv_w, conv_b, dt_bias_fwd, dt_bias_bwd, a_log_fwd, a_log_bwd, d_skip,
           ssm_norm_w, q_norm_w, k_norm_w, w_out, ln2_w, w_up, w_down, final_norm_w):
    b, s, _ = x.shape
    depth = w_in.shape[0]
    consts = _constants()
    cos_t, sin_t = _rope_tables(s)
    row = lambda v: v.reshape(1, -1).astype(F32)
    for i in range(depth):
        w_all, wdt_t, dtb, alog, dtb_t, alog_t = _layer_weights(
            w_in[i], dt_bias_fwd[i], dt_bias_bwd[i], a_log_fwd[i], a_log_bwd[i])
        qkv, z, xc, bt, dt_pad, dt_t = _in_proj(
            x, row(ln1_w[i]), w_all, wdt_t, conv_w[i].astype(F32), row(conv_b[i]))
        qt, kn, vt = _qk_prep(
            qkv, cos_t, sin_t, row(jnp.tile(q_norm_w[i], N_Q_HEADS)), row(jnp.tile(k_norm_w[i], N_KV_HEADS)),
            consts["seg"])
        score_bound = (HEAD_DIM ** 0.5 * LOG2E * jnp.max(jnp.abs(q_norm_w[i])) * jnp.max(jnp.abs(k_norm_w[i]))
                       ).astype(F32).reshape(1, 1)
        attn, y = _mixer(
            score_bound, qt, kn, vt, xc, bt, dt_pad, dt_t.reshape(b, N_SSM_GROUPS, DT_ROWS, s), dtb, alog, dtb_t, alog_t,
            row(jnp.repeat(d_skip[i], SSM_HEAD_DIM)), consts["tril"], consts["triu"])
        x = _out_mlp(
            attn, y, z, x, row(ssm_norm_w[i]),
            w_out[i, :ATTN_WIDTH].astype(BF16), w_out[i, ATTN_WIDTH:].astype(BF16),
            row(ln2_w[i]), w_up[i].astype(BF16), w_down[i].astype(BF16), row(final_norm_w),
            final_norm=(i == depth - 1))
    return x
```

```python
import functools

import numpy as np
import jax
import jax.numpy as jnp
from jax import lax
from jax.experimental import pallas as pl
from jax.experimental.pallas import tpu as pltpu

F32 = jnp.float32
BF16 = jnp.bfloat16

D_MODEL = 1024
GRID_W = 64
N_Q_HEADS = 16
N_KV_HEADS = 4
HEAD_DIM = 64
REP = N_Q_HEADS // N_KV_HEADS
ATTN_WIDTH = N_Q_HEADS * HEAD_DIM
KV_WIDTH = N_KV_HEADS * HEAD_DIM
ROPE_THETA = 10000.0
D_INNER = 2048
SSM_HEAD_DIM = 64
N_SSM_HEADS = D_INNER // SSM_HEAD_DIM
N_SSM_GROUPS = 4
HEADS_PER_GROUP = N_SSM_HEADS // N_SSM_GROUPS
GROUP_WIDTH = D_INNER // N_SSM_GROUPS
D_STATE = 128
D_CONV = 5
CHUNK = 128
CONV_DIM = D_INNER + 2 * N_SSM_GROUPS * D_STATE
D_FF = 4 * D_MODEL
NORM_EPS = 1e-5
QK_EPS = 1e-6
LOG2E = 1.4426950408889634

LANES = 128
VMEM_LIMIT_BYTES = 58 * 1024 * 1024

QKV_WIDTH = ATTN_WIDTH + 2 * KV_WIDTH
Z_OFF = QKV_WIDTH
XBC_OFF = Z_OFF + D_INNER
DT_OFF = XBC_OFF + CONV_DIM
DT_PAD = N_SSM_GROUPS * LANES
W_ALL_WIDTH = DT_OFF + DT_PAD
DT_ROWS = 2 * HEADS_PER_GROUP

TM_PROJ = 512
TS_PREP = 512
TQ = 512
KEY_CHUNK = 512
PV_CHUNKS = 2
VT_ROWS = HEAD_DIM + 16
SAFE_SHIFT = 40.0
TM_MLP = 512
COPY_ROWS = 256
HALO = 16
CONV_STRIP = 256
PLAIN_STRIP = 512
CONV_ROWS = 128


def _dot(a, b):
    return jnp.dot(a, b, preferred_element_type=F32)


def _dot_nt(a, b):
    return lax.dot_general(a, b, (((1,), (1,)), ((), ())), preferred_element_type=F32)


def _params(semantics, flags=None):
    return pltpu.CompilerParams(dimension_semantics=semantics, vmem_limit_bytes=VMEM_LIMIT_BYTES, flags=flags)


def _const_spec(shape):
    nd = len(shape)
    return pl.BlockSpec(shape, lambda *_: (0,) * nd, pipeline_mode=pl.Buffered(1))


def _silu(v):
    return v * (1.0 / (1.0 + jnp.exp(-v)))


def _inproj_kernel(x_ref, xp_ref, xn_ref, ln_ref, w_ref, wdt_t_ref, cw_ref, cb_ref,
                   qkv_ref, z_ref, xc_ref, bt_ref, dt_ref, dtt_ref, win_scr):
    j = pl.program_id(1)
    tm = x_ref.shape[1]

    def normed(x):
        ms = jnp.mean(x * x, axis=-1, keepdims=True)
        return (x * lax.rsqrt(ms + NORM_EPS) * ln_ref[...]).astype(BF16)

    hn = normed(x_ref[0])
    dtt_ref[0] = _dot_nt(wdt_t_ref[...], hn)

    def plain_job(out_ref, w_off, o_off):
        def run():
            res = _dot(hn, w_ref[:, w_off:w_off + PLAIN_STRIP])
            out_ref[0, :, o_off:o_off + PLAIN_STRIP] = res.astype(out_ref.dtype)
        return run

    plain_jobs = (
        [plain_job(qkv_ref, o, o) for o in range(0, QKV_WIDTH, PLAIN_STRIP)]
        + [plain_job(z_ref, Z_OFF + o, o) for o in range(0, D_INNER, PLAIN_STRIP)]
        + [plain_job(dt_ref, DT_OFF + o, o) for o in range(0, DT_PAD, PLAIN_STRIP)])

    hn_ext = jnp.concatenate([normed(xp_ref[0]), hn, normed(xn_ref[0])], axis=0)
    ext_row = lax.broadcasted_iota(jnp.int32, (tm + 2 * HALO, 1), 0)
    outside = ((ext_row < HALO) & (j == 0)) | ((ext_row >= HALO + tm) & (j == pl.num_programs(1) - 1))
    pad = D_CONV // 2
    n_strips = CONV_DIM // CONV_STRIP
    project = lambda t: _dot(hn_ext, w_ref[:, XBC_OFF + t * CONV_STRIP:XBC_OFF + (t + 1) * CONV_STRIP])
    nxt = project(0)
    for t in range(n_strips):
        cur = nxt
        if t + 1 < n_strips:
            nxt = project(t + 1)
        if plain_jobs:
            plain_jobs.pop(0)()
        win = win_scr.at[t % 2]
        win[...] = jnp.where(outside, 0.0, cur)
        cols = slice(t * CONV_STRIP, (t + 1) * CONV_STRIP)
        b_lo = D_INNER // CONV_STRIP
        b_hi = b_lo + N_SSM_GROUPS * D_STATE // CONV_STRIP
        for r0 in range(0, tm, CONV_ROWS):
            acc = win[HALO - pad + r0:HALO - pad + r0 + CONV_ROWS, :] * cw_ref[0:1, cols] + cb_ref[:, cols]
            for k in range(1, D_CONV):
                off = HALO - pad + k + r0
                acc = acc + win[off:off + CONV_ROWS, :] * cw_ref[k:k + 1, cols]
            y = _silu(acc)
            xc_ref[0, r0:r0 + CONV_ROWS, cols] = y.astype(BF16)
            if b_lo <= t < b_hi:
                bt_ref[0, (t - b_lo) * CONV_STRIP:(t - b_lo + 1) * CONV_STRIP, r0:r0 + CONV_ROWS] = (
                    y.T.astype(BF16))
    for job in plain_jobs:
        job()


def _in_proj(x, ln_w, w_all, wdt_t, conv_w, conv_b):
    b, s, _ = x.shape
    tm = min(TM_PROJ, s)
    grid = (b, s // tm)
    halo_blocks = tm // HALO
    last_halo_block = s // HALO - 1
    tok = lambda width: pl.BlockSpec((1, tm, width), lambda i, j: (i, j, 0))
    bn = N_SSM_GROUPS * D_STATE
    return pl.pallas_call(
        _inproj_kernel,
        grid=grid,
        in_specs=[
            tok(D_MODEL),
            pl.BlockSpec((1, HALO, D_MODEL), lambda i, j: (i, jnp.maximum(j * halo_blocks - 1, 0), 0)),
            pl.BlockSpec((1, HALO, D_MODEL),
                         lambda i, j: (i, jnp.minimum((j + 1) * halo_blocks, last_halo_block), 0)),
            _const_spec((1, D_MODEL)),
            _const_spec((D_MODEL, W_ALL_WIDTH)),
            _const_spec((N_SSM_GROUPS * DT_ROWS, D_MODEL)),
            _const_spec((D_CONV, CONV_DIM)),
            _const_spec((1, CONV_DIM)),
        ],
        out_specs=[
            tok(QKV_WIDTH), tok(D_INNER), tok(CONV_DIM),
            pl.BlockSpec((1, bn, tm), lambda i, j: (i, 0, j)),
            tok(DT_PAD),
            pl.BlockSpec((1, N_SSM_GROUPS * DT_ROWS, tm), lambda i, j: (i, 0, j)),
        ],
        out_shape=[
            jax.ShapeDtypeStruct((b, s, QKV_WIDTH), BF16),
            jax.ShapeDtypeStruct((b, s, D_INNER), BF16),
            jax.ShapeDtypeStruct((b, s, CONV_DIM), BF16),
            jax.ShapeDtypeStruct((b, bn, s), BF16),
            jax.ShapeDtypeStruct((b, s, DT_PAD), F32),
            jax.ShapeDtypeStruct((b, N_SSM_GROUPS * DT_ROWS, s), F32),
        ],
        scratch_shapes=[pltpu.VMEM((2, tm + 2 * HALO, CONV_STRIP), F32)],
        compiler_params=_params(("parallel", "parallel")),
        name="in_proj",
    )(x, x, x, ln_w, w_all, wdt_t, conv_w, conv_b)


def _norm_rope(t, w, seg_ones, cos, sin_signed, post_scale):
    width = t.shape[1]
    ss = _dot((t * t).astype(BF16), seg_ones) * (1.0 / HEAD_DIM)
    y = t * lax.rsqrt(ss + QK_EPS) * w
    reps = width // LANES
    cosw = jnp.concatenate([cos] * reps, axis=1)
    sinw = jnp.concatenate([sin_signed] * reps, axis=1)
    quarter = HEAD_DIM // 4
    upper = pltpu.roll(y, width - quarter, 1)
    lower = pltpu.roll(y, quarter, 1)
    lane = lax.broadcasted_iota(jnp.int32, y.shape, 1)
    first_half = (lane % (2 * quarter)) < quarter
    rot = jnp.where(first_half, upper, lower)
    return (y * cosw + rot * sinw) * post_scale


def _qkprep_kernel(q_ref, k_ref, v_ref, cos_ref, sin_ref, qw_ref, kw_ref, seg_ref,
                   qt_ref, kn_ref, vt_ref):
    cos = cos_ref[...]
    sin = sin_ref[...]
    q = _norm_rope(q_ref[0].astype(F32), qw_ref[...], seg_ref[...], cos, sin,
                   (HEAD_DIM ** -0.5) * LOG2E)
    qt_ref[0] = q.T.astype(BF16)
    k = _norm_rope(k_ref[0].astype(F32), kw_ref[...], seg_ref[0:KV_WIDTH, 0:KV_WIDTH], cos, sin, 1.0)
    kn_ref[0] = k.astype(BF16)
    vt = v_ref[0].astype(F32).T.astype(BF16)
    ones = jnp.ones((VT_ROWS - HEAD_DIM, vt.shape[1]), BF16)
    vt_ref[0] = jnp.concatenate(
        [piece for g in range(N_KV_HEADS) for piece in (vt[g * HEAD_DIM:(g + 1) * HEAD_DIM], ones)], axis=0)


def _qk_prep(qkv, cos_t, sin_t, qw, kw, seg_ones):
    b, s, _ = qkv.shape
    ts = min(TS_PREP, s)
    grid = (b, s // ts)
    return pl.pallas_call(
        _qkprep_kernel,
        grid=grid,
        in_specs=[
            pl.BlockSpec((1, ts, ATTN_WIDTH), lambda i, j: (i, j, 0)),
            pl.BlockSpec((1, ts, KV_WIDTH), lambda i, j: (i, j, ATTN_WIDTH // KV_WIDTH)),
            pl.BlockSpec((1, ts, KV_WIDTH), lambda i, j: (i, j, ATTN_WIDTH // KV_WIDTH + 1)),
            pl.BlockSpec((ts, LANES), lambda i, j: (j, 0)),
            pl.BlockSpec((ts, LANES), lambda i, j: (j, 0)),
            _const_spec((1, ATTN_WIDTH)),
            _const_spec((1, KV_WIDTH)),
            _const_spec((ATTN_WIDTH, ATTN_WIDTH)),
        ],
        out_specs=[
            pl.BlockSpec((1, ATTN_WIDTH, ts), lambda i, j: (i, 0, j)),
            pl.BlockSpec((1, ts, KV_WIDTH), lambda i, j: (i, j, 0)),
            pl.BlockSpec((1, N_KV_HEADS * VT_ROWS, ts), lambda i, j: (i, 0, j)),
        ],
        out_shape=[
            jax.ShapeDtypeStruct((b, ATTN_WIDTH, s), BF16),
            jax.ShapeDtypeStruct((b, s, KV_WIDTH), BF16),
            jax.ShapeDtypeStruct((b, N_KV_HEADS * VT_ROWS, s), BF16),
        ],
        compiler_params=_params(("parallel", "parallel")),
        name="qk_prep",
    )(qkv, qkv, qkv, cos_t, sin_t, qw, kw, seg_ones)


def _attn_setup(qt_ref, k_ref, bound_ref, qe_scr, shift_scr):
    g = pl.program_id(1)
    s_len = k_ref.shape[1]
    kc = min(KEY_CHUNK, s_len)
    qt = qt_ref[0]
    q_cols = jnp.concatenate([qt[r * HEAD_DIM:(r + 1) * HEAD_DIM, :] for r in range(REP)], axis=1)
    @pl.when(pl.program_id(2) == 0)
    def _():
        qe_scr[...] = jnp.zeros_like(qe_scr)

    qe_scr[pl.ds(pl.multiple_of(g * HEAD_DIM, HEAD_DIM), HEAD_DIM), :] = q_cols
    bound = bound_ref[0, 0]

    @pl.when(bound <= SAFE_SHIFT)
    def _():
        shift_scr[...] = jnp.full(shift_scr.shape, bound, F32)

    @pl.when(bound > SAFE_SHIFT)
    def _():
        def body(c, mx):
            r0 = pl.multiple_of(c * kc, kc)
            st = _dot(k_ref[0, pl.ds(r0, kc), :], qe_scr[...])
            return jnp.maximum(mx, jnp.max(st, axis=0, keepdims=True))

        shift_scr[...] = lax.fori_loop(0, s_len // kc, body, jnp.full(shift_scr.shape, -jnp.inf, F32))


def _attn_stages(k_ref, vt_ref, o_ref, qe_scr, shift_scr, p_scr):
    tq = o_ref.shape[1]
    s_len = k_ref.shape[1]
    kc = min(KEY_CHUNK, s_len)
    n_kc = s_len // kc
    n_cols = REP * tq
    acc = jnp.zeros((VT_ROWS, n_cols), F32)
    group = min(PV_CHUNKS, n_kc)
    scores = lambda c: _dot(k_ref[0, c * kc:(c + 1) * kc, :], qe_scr[...])
    st_next = scores(0)
    for c in range(n_kc + 1):
        st = st_next
        if c + 1 < n_kc:
            st_next = scores(c + 1)
        if c > 0 and c % group == 0:
            first = c - group
            acc = acc + _dot(vt_ref[0, :, first * kc:c * kc], p_scr[(first // group) % 2])
        yield
        if c < n_kc:
            p_scr[(c // group) % 2, (c % group) * kc:(c % group + 1) * kc, :] = (
                jnp.exp2(st - shift_scr[...]).astype(BF16))
            yield
    ot = (acc[:HEAD_DIM] / acc[HEAD_DIM:HEAD_DIM + 1]).T
    out = jnp.concatenate([ot[r * tq:(r + 1) * tq, :] for r in range(REP)], axis=1)
    o_ref[0] = out.astype(o_ref.dtype)


def _split_hi_lo(v):
    hi = v.astype(BF16)
    lo = (v - hi.astype(F32)).astype(BF16)
    return hi, lo


def _softplus(v):
    return jnp.maximum(v, 0.0) + jnp.log1p(jnp.exp(-jnp.abs(v)))


def _mixer_kernel(bound_ref, qt_ref, k_ref, vt_ref,
                  xc_ref, b_ref, bt_ref, c_ref, dt_ref, dtt_ref,
                  dtb_ref, alog_ref, dtbt_ref, alogt_ref, dskip_ref,
                  tril_ref, triu_ref,
                  attn_ref, y_ref,
                  qe_scr, shift_scr, p_scr, y_scr, stf_scr, stb_scr):
    _attn_setup(qt_ref, k_ref, bound_ref, qe_scr, shift_scr)
    j = pl.program_id(2)
    n_blocks = pl.num_programs(2)
    s_len = xc_ref.shape[1]
    n_chunks = s_len // CHUNK
    iters = n_chunks // (s_len // qt_ref.shape[2])
    hp = HEADS_PER_GROUP
    rows = min(COPY_ROWS, s_len)

    @pl.when(j == 0)
    def _():
        def skip_body(i, carry):
            r0 = pl.multiple_of(i * rows, rows)
            y_scr[pl.ds(r0, rows), :] = dskip_ref[...] * xc_ref[0, pl.ds(r0, rows), :].astype(F32)
            return carry

        lax.fori_loop(0, s_len // rows, skip_body, 0)
        stf_scr[...] = jnp.zeros_like(stf_scr)
        stb_scr[...] = jnp.zeros_like(stb_scr)

    row = lax.broadcasted_iota(jnp.int32, (CHUNK, CHUNK), 0)
    col = lax.broadcasted_iota(jnp.int32, (CHUNK, CHUNK), 1)
    lane = lax.broadcasted_iota(jnp.int32, (CHUNK, LANES), 1)
    lane_lt_half = lane < SSM_HEAD_DIM
    neg_a = -jnp.exp(alog_ref[0]) * LOG2E
    neg_a_t = -jnp.exp(alogt_ref[0]) * LOG2E

    def chunk_step(forward, c, st_scr):
        tri = tril_ref[...] if forward else triu_ref[...]
        tri_t = triu_ref[...] if forward else tril_ref[...]
        keep = (row >= col) if forward else (row <= col)
        lane0 = 0 if forward else hp
        last = CHUNK - 1 if forward else 0
        r0 = pl.multiple_of(c * CHUNK, CHUNK)
        bc = b_ref[0, pl.ds(r0, CHUNK), :]
        bt = bt_ref[0, :, pl.ds(r0, CHUNK)]
        cc = c_ref[0, pl.ds(r0, CHUNK), :]
        dts = _softplus(dt_ref[0, pl.ds(r0, CHUNK), :] + dtb_ref[0])
        a_hi, a_lo = _split_hi_lo(dts * neg_a)
        a_cat = jnp.concatenate([a_hi, a_lo], axis=1)
        dts_t = _softplus(dtt_ref[0, 0, :, pl.ds(r0, CHUNK)] + dtbt_ref[0])
        at_hi, at_lo = _split_hi_lo(dts_t * neg_a_t)
        at_cat = jnp.concatenate([at_hi, at_lo], axis=0)
        yield
        acs2 = _dot(tri, a_cat)
        acst2 = _dot(at_cat, tri_t)
        cb = _dot_nt(cc, bc)
        yield
        acs = acs2[:, :LANES] + acs2[:, LANES:]
        acs_t = acst2[:DT_ROWS] + acst2[DT_ROWS:]
        ms, dt_tiles, acs_tiles = [], [], []
        for j in range(hp // 2):
            pair, dt_cols, acs_cols = [], [], []
            for h in (2 * j, 2 * j + 1):
                acs_col = jnp.broadcast_to(acs[:, lane0 + h:lane0 + h + 1], (CHUNK, LANES))
                dt_cols.append(jnp.broadcast_to(dts[:, lane0 + h:lane0 + h + 1], (CHUNK, LANES)))
                acs_cols.append(acs_col)
                seg = acs_col - acs_t[lane0 + h:lane0 + h + 1, :]
                lmat = jnp.exp2(jnp.where(keep, seg, -jnp.inf))
                pair.append((cb * lmat).astype(BF16))
            ms.append(jnp.concatenate(pair, axis=1))
            dt_tiles.append(jnp.where(lane_lt_half, dt_cols[0], dt_cols[1]))
            acs_tiles.append(jnp.where(lane_lt_half, acs_cols[0], acs_cols[1]))
        dt_exp = jnp.concatenate(dt_tiles, axis=1)
        acs_exp = jnp.concatenate(acs_tiles, axis=1)
        acs_last = acs_exp[last:last + 1, :]
        xdt = xc_ref[0, pl.ds(r0, CHUNK), :].astype(F32) * dt_exp
        xdt_b = xdt.astype(BF16)
        xd = (xdt * jnp.exp2(acs_last - acs_exp)).astype(BF16)
        rhs = []
        for j in range(hp // 2):
            xp = xdt_b[:, j * LANES:(j + 1) * LANES]
            zero = jnp.zeros_like(xp)
            rhs.append(jnp.concatenate([jnp.where(lane_lt_half, xp, zero),
                                        jnp.where(lane_lt_half, zero, xp)], axis=0))
        yield
        state = st_scr[...]
        y_off = _dot(cc, state.astype(BF16))
        st_new = _dot(bt, xd)
        pairs = [_dot(ms[j], rhs[j]) for j in range(hp // 2)]
        yield
        st_scr[...] = state * jnp.exp2(acs_last) + st_new
        y_scr[pl.ds(r0, CHUNK), :] += y_off * jnp.exp2(acs_exp) + jnp.concatenate(pairs, axis=1)

    def scan_stages():
        for it in range(iters):
            i = j * iters + it
            active = [chunk_step(True, i, stf_scr), chunk_step(False, n_chunks - 1 - i, stb_scr)]
            while active:
                active = [gen for gen in active if next(gen, True) is None]
                yield

    streams = [_attn_stages(k_ref, vt_ref, attn_ref, qe_scr, shift_scr, p_scr), scan_stages()]
    while streams:
        streams = [gen for gen in streams if next(gen, True) is None]

    @pl.when(j == n_blocks - 1)
    def _():
        def out_body(i, carry):
            r0 = pl.multiple_of(i * rows, rows)
            y_ref[0, pl.ds(r0, rows), :] = y_scr[pl.ds(r0, rows), :].astype(y_ref.dtype)
            return carry

        lax.fori_loop(0, s_len // rows, out_body, 0)


def _mixer(score_bound, qt, kn, vt, xc, bt, dt_pad, dt_t, dtb, alog, dtb_t, alog_t, dskip, tril, triu):
    b, s, _ = xc.shape
    tq = min(TQ, s)
    assert (s // CHUNK) % (s // tq) == 0
    grid = (b, N_SSM_GROUPS, s // tq)
    gw = GROUP_WIDTH
    aw = REP * HEAD_DIM
    b_block0 = D_INNER // D_STATE
    c_block0 = b_block0 + N_SSM_GROUPS
    once = pl.Buffered(1)
    seq = lambda width, off: pl.BlockSpec((1, s, width), lambda i, g, j: (i, 0, off + g), pipeline_mode=once)
    gspec = lambda rows: pl.BlockSpec((1, rows, LANES), lambda i, g, j: (g, 0, 0))
    return pl.pallas_call(
        _mixer_kernel,
        grid=grid,
        in_specs=[
            pl.BlockSpec(memory_space=pltpu.SMEM),
            pl.BlockSpec((1, aw, tq), lambda i, g, j: (i, g, j)),
            pl.BlockSpec((1, s, KV_WIDTH), lambda i, g, j: (i, 0, 0)),
            pl.BlockSpec((1, VT_ROWS, s), lambda i, g, j: (i, g, 0)),
            seq(gw, 0), seq(D_STATE, b_block0),
            pl.BlockSpec((1, D_STATE, s), lambda i, g, j: (i, g, 0), pipeline_mode=once),
            seq(D_STATE, c_block0),
            seq(LANES, 0),
            pl.BlockSpec((1, 1, DT_ROWS, s), lambda i, g, j: (i, g, 0, 0)),
            gspec(1), gspec(1), gspec(DT_ROWS), gspec(DT_ROWS),
            pl.BlockSpec((1, gw), lambda i, g, j: (0, g)),
            _const_spec((CHUNK, CHUNK)), _const_spec((CHUNK, CHUNK)),
        ],
        out_specs=[
            pl.BlockSpec((1, tq, aw), lambda i, g, j: (i, j, g)),
            pl.BlockSpec((1, s, gw), lambda i, g, j: (i, 0, g), pipeline_mode=once),
        ],
        out_shape=[
            jax.ShapeDtypeStruct((b, s, ATTN_WIDTH), BF16),
            jax.ShapeDtypeStruct((b, s, D_INNER), BF16),
        ],
        scratch_shapes=[
            pltpu.VMEM((KV_WIDTH, REP * tq), BF16),
            pltpu.VMEM((1, REP * tq), F32),
            pltpu.VMEM((2, min(PV_CHUNKS * KEY_CHUNK, s), REP * tq), BF16),
            pltpu.VMEM((s, gw), F32),
            pltpu.VMEM((D_STATE, gw), F32),
            pltpu.VMEM((D_STATE, gw), F32),
        ],
        compiler_params=_params(("parallel", "arbitrary", "arbitrary")),
        name="mixer",
    )(score_bound, qt, kn, vt, xc, xc, bt, xc, dt_pad, dt_t, dtb, alog, dtb_t, alog_t, dskip, tril, triu)


def _outmlp_kernel(attn_ref, y_ref, z_ref, x_ref, nw_ref, woa_ref, wos_ref, ln2_ref, wup_ref, wdn_ref,
                   fin_ref, o_ref, *, final_norm):
    gated = []
    for gi in range(N_SSM_GROUPS):
        cols = slice(gi * GROUP_WIDTH, (gi + 1) * GROUP_WIDTH)
        yg = y_ref[0, :, cols].astype(F32) * _silu(z_ref[0, :, cols].astype(F32))
        msg = jnp.mean(yg * yg, axis=-1, keepdims=True)
        gated.append((yg * lax.rsqrt(msg + NORM_EPS) * nw_ref[:, cols]).astype(BF16))
    ssm = jnp.concatenate(gated, axis=1)
    x1 = x_ref[0] + _dot(attn_ref[0], woa_ref[...]) + _dot(ssm, wos_ref[...])
    ms = jnp.mean(x1 * x1, axis=-1, keepdims=True)
    h = (x1 * lax.rsqrt(ms + NORM_EPS) * ln2_ref[...]).astype(BF16)
    u = jnp.maximum(_dot(h, wup_ref[...]), 0.0)
    x2 = x1 + _dot((u * u).astype(BF16), wdn_ref[...])
    if final_norm:
        ms2 = jnp.mean(x2 * x2, axis=-1, keepdims=True)
        x2 = x2 * lax.rsqrt(ms2 + NORM_EPS) * fin_ref[...]
    o_ref[0] = x2


def _out_mlp(attn, y, z, x, norm_w, wo_attn, wo_ssm, ln2, w_up, w_down, fin_w, final_norm):
    b, s, _ = x.shape
    tm = min(TM_MLP, s)
    grid = (b, s // tm)
    tok = lambda width: pl.BlockSpec((1, tm, width), lambda i, j: (i, j, 0))
    return pl.pallas_call(
        functools.partial(_outmlp_kernel, final_norm=final_norm),
        grid=grid,
        in_specs=[
            tok(ATTN_WIDTH), tok(D_INNER), tok(D_INNER), tok(D_MODEL),
            _const_spec((1, D_INNER)),
            _const_spec((ATTN_WIDTH, D_MODEL)), _const_spec((D_INNER, D_MODEL)),
            _const_spec((1, D_MODEL)),
            _const_spec((D_MODEL, D_FF)), _const_spec((D_FF, D_MODEL)),
            _const_spec((1, D_MODEL)),
        ],
        out_specs=tok(D_MODEL),
        out_shape=jax.ShapeDtypeStruct((b, s, D_MODEL), F32),
        compiler_params=_params(("parallel", "parallel")),
        name="out_mlp",
    )(attn, y, z, x, norm_w, wo_attn, wo_ssm, ln2, w_up, w_down, fin_w)


def _rope_tables(seq):
    rows = seq // GRID_W
    row_ids = jnp.repeat(jnp.arange(rows, dtype=jnp.int32), GRID_W)
    col_ids = jnp.tile(jnp.arange(GRID_W, dtype=jnp.int32), rows)
    half = HEAD_DIM // 2
    inv_freq = ROPE_THETA ** (-jnp.arange(0, half, 2, dtype=F32) / half)

    def ang(pos):
        a = pos.astype(F32)[:, None] * inv_freq[None, :]
        return jnp.concatenate([a, a], axis=-1)

    a = jnp.concatenate([ang(row_ids), ang(col_ids)], axis=-1)
    sign = jnp.tile(jnp.concatenate([-jnp.ones((half // 2,), F32), jnp.ones((half // 2,), F32)]), 2)
    cos = jnp.cos(a)
    sin = jnp.sin(a) * sign[None, :]
    return jnp.tile(cos, (1, LANES // HEAD_DIM)), jnp.tile(sin, (1, LANES // HEAD_DIM))


def _constants():
    seg = np.kron(np.eye(N_Q_HEADS, dtype=np.float32), np.ones((HEAD_DIM, HEAD_DIM), np.float32))
    tril = np.tril(np.ones((CHUNK, CHUNK), np.float32))
    return dict(
        seg=jnp.asarray(seg, BF16),
        tril=jnp.asarray(tril, BF16), triu=jnp.asarray(tril.T, BF16),
    )


def _group_rows(vf, vb):
    return jnp.concatenate([vf.reshape(N_SSM_GROUPS, HEADS_PER_GROUP),
                            vb.reshape(N_SSM_GROUPS, HEADS_PER_GROUP)], axis=1)


def _layer_weights(w_in, dt_bias_f, dt_bias_b, a_log_f, a_log_b):
    wdt = w_in[:, XBC_OFF + CONV_DIM:]
    wf = wdt[:, :N_SSM_HEADS].reshape(D_MODEL, N_SSM_GROUPS, HEADS_PER_GROUP)
    wb = wdt[:, N_SSM_HEADS:].reshape(D_MODEL, N_SSM_GROUPS, HEADS_PER_GROUP)
    grp = jnp.concatenate([wf, wb], axis=-1)
    wdt_pad = jnp.pad(grp, ((0, 0), (0, 0), (0, LANES - DT_ROWS))).reshape(D_MODEL, DT_PAD)
    w_all = jnp.concatenate([w_in[:, :DT_OFF], wdt_pad], axis=1).astype(BF16)
    wdt_t = grp.reshape(D_MODEL, N_SSM_GROUPS * DT_ROWS).T.astype(BF16)
    pad_lanes = lambda v: jnp.pad(v, ((0, 0), (0, LANES - DT_ROWS)))[:, None, :]
    dtb = _group_rows(dt_bias_f, dt_bias_b)
    alog = _group_rows(a_log_f, a_log_b)
    bcast = lambda v: jnp.broadcast_to(v[:, :, None], (N_SSM_GROUPS, DT_ROWS, LANES))
    return w_all, wdt_t, pad_lanes(dtb), pad_lanes(alog), bcast(dtb), bcast(alog)


def kernel(x, ln1_w, w_in, conv_w, conv_b, dt_bias_fwd, dt_bias_bwd, a_log_fwd, a_log_bwd, d_skip,
           ssm_norm_w, q_norm_w, k_norm_w, w_out, ln2_w, w_up, w_down, final_norm_w):
    b, s, _ = x.shape
    depth = w_in.shape[0]
    consts = _constants()
    cos_t, sin_t = _rope_tables(s)
    row = lambda v: v.reshape(1, -1).astype(F32)
    for i in range(depth):
        w_all, wdt_t, dtb, alog, dtb_t, alog_t = _layer_weights(
            w_in[i], dt_bias_fwd[i], dt_bias_bwd[i], a_log_fwd[i], a_log_bwd[i])
        qkv, z, xc, bt, dt_pad, dt_t = _in_proj(
            x, row(ln1_w[i]), w_all, wdt_t, conv_w[i].astype(F32), row(conv_b[i]))
        qt, kn, vt = _qk_prep(
            qkv, cos_t, sin_t, row(jnp.tile(q_norm_w[i], N_Q_HEADS)), row(jnp.tile(k_norm_w[i], N_KV_HEADS)),
            consts["seg"])
        score_bound = (HEAD_DIM ** 0.5 * LOG2E * jnp.max(jnp.abs(q_norm_w[i])) * jnp.max(jnp.abs(k_norm_w[i]))
                       ).astype(F32).reshape(1, 1)
        attn, y = _mixer(
            score_bound, qt, kn, vt, xc, bt, dt_pad, dt_t.reshape(b, N_SSM_GROUPS, DT_ROWS, s), dtb, alog, dtb_t, alog_t,
            row(jnp.repeat(d_skip[i], SSM_HEAD_DIM)), consts["tril"], consts["triu"])
        x = _out_mlp(
            attn, y, z, x, row(ssm_norm_w[i]),
            w_out[i, :ATTN_WIDTH].astype(BF16), w_out[i, ATTN_WIDTH:].astype(BF16),
            row(ln2_w[i]), w_up[i].astype(BF16), w_down[i].astype(BF16), row(final_norm_w),
            final_norm=(i == depth - 1))
    return x
```

```python
import functools

import numpy as np
import jax
import jax.numpy as jnp
from jax import lax
from jax.experimental import pallas as pl
from jax.experimental.pallas import tpu as pltpu

F32 = jnp.float32
BF16 = jnp.bfloat16

D_MODEL = 1024
GRID_W = 64
N_Q_HEADS = 16
N_KV_HEADS = 4
HEAD_DIM = 64
REP = N_Q_HEADS // N_KV_HEADS
ATTN_WIDTH = N_Q_HEADS * HEAD_DIM
KV_WIDTH = N_KV_HEADS * HEAD_DIM
ROPE_THETA = 10000.0
D_INNER = 2048
SSM_HEAD_DIM = 64
N_SSM_HEADS = D_INNER // SSM_HEAD_DIM
N_SSM_GROUPS = 4
HEADS_PER_GROUP = N_SSM_HEADS // N_SSM_GROUPS
GROUP_WIDTH = D_INNER // N_SSM_GROUPS
D_STATE = 128
D_CONV = 5
CHUNK = 128
CONV_DIM = D_INNER + 2 * N_SSM_GROUPS * D_STATE
D_FF = 4 * D_MODEL
NORM_EPS = 1e-5
QK_EPS = 1e-6
LOG2E = 1.4426950408889634

LANES = 128
VMEM_LIMIT_BYTES = 58 * 1024 * 1024

QKV_WIDTH = ATTN_WIDTH + 2 * KV_WIDTH
Z_OFF = QKV_WIDTH
XBC_OFF = Z_OFF + D_INNER
DT_OFF = XBC_OFF + CONV_DIM
DT_PAD = N_SSM_GROUPS * LANES
W_ALL_WIDTH = DT_OFF + DT_PAD
DT_ROWS = 2 * HEADS_PER_GROUP

TM_PROJ = 512
TS_PREP = 512
TQ = 512
KEY_CHUNK = 512
PV_CHUNKS = 2
VT_ROWS = HEAD_DIM + 16
SAFE_SHIFT = 40.0
TM_MLP = 512
COPY_ROWS = 256
HALO = 16
CONV_STRIP = 256
PLAIN_STRIP = 512
CONV_ROWS = 128


def _dot(a, b):
    return jnp.dot(a, b, preferred_element_type=F32)


def _dot_nt(a, b):
    return lax.dot_general(a, b, (((1,), (1,)), ((), ())), preferred_element_type=F32)


def _params(semantics, flags=None):
    return pltpu.CompilerParams(dimension_semantics=semantics, vmem_limit_bytes=VMEM_LIMIT_BYTES, flags=flags)


def _const_spec(shape):
    nd = len(shape)
    return pl.BlockSpec(shape, lambda *_: (0,) * nd, pipeline_mode=pl.Buffered(1))


def _silu(v):
    return v * (1.0 / (1.0 + jnp.exp(-v)))


def _inproj_kernel(x_ref, xp_ref, xn_ref, ln_ref, w_ref, wdt_t_ref, cw_ref, cb_ref,
                   qkv_ref, z_ref, xc_ref, bt_ref, dt_ref, dtt_ref, win_scr):
    j = pl.program_id(1)
    tm = x_ref.shape[1]

    def normed(x):
        ms = jnp.mean(x * x, axis=-1, keepdims=True)
        return (x * lax.rsqrt(ms + NORM_EPS) * ln_ref[...]).astype(BF16)

    hn = normed(x_ref[0])
    dtt_ref[0] = _dot_nt(wdt_t_ref[...], hn)

    def plain_job(out_ref, w_off, o_off):
        def run():
            res = _dot(hn, w_ref[:, w_off:w_off + PLAIN_STRIP])
            out_ref[0, :, o_off:o_off + PLAIN_STRIP] = res.astype(out_ref.dtype)
        return run

    plain_jobs = (
        [plain_job(qkv_ref, o, o) for o in range(0, QKV_WIDTH, PLAIN_STRIP)]
        + [plain_job(z_ref, Z_OFF + o, o) for o in range(0, D_INNER, PLAIN_STRIP)]
        + [plain_job(dt_ref, DT_OFF + o, o) for o in range(0, DT_PAD, PLAIN_STRIP)])

    hn_ext = jnp.concatenate([normed(xp_ref[0]), hn, normed(xn_ref[0])], axis=0)
    ext_row = lax.broadcasted_iota(jnp.int32, (tm + 2 * HALO, 1), 0)
    outside = ((ext_row < HALO) & (j == 0)) | ((ext_row >= HALO + tm) & (j == pl.num_programs(1) - 1))
    pad = D_CONV // 2
    n_strips = CONV_DIM // CONV_STRIP
    project = lambda t: _dot(hn_ext, w_ref[:, XBC_OFF + t * CONV_STRIP:XBC_OFF + (t + 1) * CONV_STRIP])
    nxt = project(0)
    for t in range(n_strips):
        cur = nxt
        if t + 1 < n_strips:
            nxt = project(t + 1)
        if plain_jobs:
            plain_jobs.pop(0)()
        win = win_scr.at[t % 2]
        win[...] = jnp.where(outside, 0.0, cur)
        cols = slice(t * CONV_STRIP, (t + 1) * CONV_STRIP)
        b_lo = D_INNER // CONV_STRIP
        b_hi = b_lo + N_SSM_GROUPS * D_STATE // CONV_STRIP
        for r0 in range(0, tm, CONV_ROWS):
            acc = win[HALO - pad + r0:HALO - pad + r0 + CONV_ROWS, :] * cw_ref[0:1, cols] + cb_ref[:, cols]
            for k in range(1, D_CONV):
                off = HALO - pad + k + r0
                acc = acc + win[off:off + CONV_ROWS, :] * cw_ref[k:k + 1, cols]
            y = _silu(acc)
            xc_ref[0, r0:r0 + CONV_ROWS, cols] = y.astype(BF16)
            if b_lo <= t < b_hi:
                bt_ref[0, (t - b_lo) * CONV_STRIP:(t - b_lo + 1) * CONV_STRIP, r0:r0 + CONV_ROWS] = (
                    y.T.astype(BF16))
    for job in plain_jobs:
        job()


def _in_proj(x, ln_w, w_all, wdt_t, conv_w, conv_b):
    b, s, _ = x.shape
    tm = min(TM_PROJ, s)
    grid = (b, s // tm)
    halo_blocks = tm // HALO
    last_halo_block = s // HALO - 1
    tok = lambda width: pl.BlockSpec((1, tm, width), lambda i, j: (i, j, 0))
    bn = N_SSM_GROUPS * D_STATE
    return pl.pallas_call(
        _inproj_kernel,
        grid=grid,
        in_specs=[
            tok(D_MODEL),
            pl.BlockSpec((1, HALO, D_MODEL), lambda i, j: (i, jnp.maximum(j * halo_blocks - 1, 0), 0)),
            pl.BlockSpec((1, HALO, D_MODEL),
                         lambda i, j: (i, jnp.minimum((j + 1) * halo_blocks, last_halo_block), 0)),
            _const_spec((1, D_MODEL)),
            _const_spec((D_MODEL, W_ALL_WIDTH)),
            _const_spec((N_SSM_GROUPS * DT_ROWS, D_MODEL)),
            _const_spec((D_CONV, CONV_DIM)),
            _const_spec((1, CONV_DIM)),
        ],
        out_specs=[
            tok(QKV_WIDTH), tok(D_INNER), tok(CONV_DIM),
            pl.BlockSpec((1, bn, tm), lambda i, j: (i, 0, j)),
            tok(DT_PAD),
            pl.BlockSpec((1, N_SSM_GROUPS * DT_ROWS, tm), lambda i, j: (i, 0, j)),
        ],
        out_shape=[
            jax.ShapeDtypeStruct((b, s, QKV_WIDTH), BF16),
            jax.ShapeDtypeStruct((b, s, D_INNER), BF16),
            jax.ShapeDtypeStruct((b, s, CONV_DIM), BF16),
            jax.ShapeDtypeStruct((b, bn, s), BF16),
            jax.ShapeDtypeStruct((b, s, DT_PAD), F32),
            jax.ShapeDtypeStruct((b, N_SSM_GROUPS * DT_ROWS, s), F32),
        ],
        scratch_shapes=[pltpu.VMEM((2, tm + 2 * HALO, CONV_STRIP), F32)],
        compiler_params=_params(("parallel", "parallel")),
        name="in_proj",
    )(x, x, x, ln_w, w_all, wdt_t, conv_w, conv_b)


def _norm_rope(t, w, seg_ones, cos, sin_signed, post_scale):
    width = t.shape[1]
    ss = _dot((t * t).astype(BF16), seg_ones) * (1.0 / HEAD_DIM)
    y = t * lax.rsqrt(ss + QK_EPS) * w
    reps = width // LANES
    cosw = jnp.concatenate([cos] * reps, axis=1)
    sinw = jnp.concatenate([sin_signed] * reps, axis=1)
    quarter = HEAD_DIM // 4
    upper = pltpu.roll(y, width - quarter, 1)
    lower = pltpu.roll(y, quarter, 1)
    lane = lax.broadcasted_iota(jnp.int32, y.shape, 1)
    first_half = (lane % (2 * quarter)) < quarter
    rot = jnp.where(first_half, upper, lower)
    return (y * cosw + rot * sinw) * post_scale


def _qkprep_kernel(q_ref, k_ref, v_ref, cos_ref, sin_ref, qw_ref, kw_ref, seg_ref,
                   qt_ref, kn_ref, vt_ref):
    cos = cos_ref[...]
    sin = sin_ref[...]
    q = _norm_rope(q_ref[0].astype(F32), qw_ref[...], seg_ref[...], cos, sin,
                   (HEAD_DIM ** -0.5) * LOG2E)
    qt_ref[0] = q.T.astype(BF16)
    k = _norm_rope(k_ref[0].astype(F32), kw_ref[...], seg_ref[0:KV_WIDTH, 0:KV_WIDTH], cos, sin, 1.0)
    kn_ref[0] = k.astype(BF16)
    vt = v_ref[0].astype(F32).T.astype(BF16)
    ones = jnp.ones((VT_ROWS - HEAD_DIM, vt.shape[1]), BF16)
    vt_ref[0] = jnp.concatenate(
        [piece for g in range(N_KV_HEADS) for piece in (vt[g * HEAD_DIM:(g + 1) * HEAD_DIM], ones)], axis=0)


def _qk_prep(qkv, cos_t, sin_t, qw, kw, seg_ones):
    b, s, _ = qkv.shape
    ts = min(TS_PREP, s)
    grid = (b, s // ts)
    return pl.pallas_call(
        _qkprep_kernel,
        grid=grid,
        in_specs=[
            pl.BlockSpec((1, ts, ATTN_WIDTH), lambda i, j: (i, j, 0)),
            pl.BlockSpec((1, ts, KV_WIDTH), lambda i, j: (i, j, ATTN_WIDTH // KV_WIDTH)),
            pl.BlockSpec((1, ts, KV_WIDTH), lambda i, j: (i, j, ATTN_WIDTH // KV_WIDTH + 1)),
            pl.BlockSpec((ts, LANES), lambda i, j: (j, 0)),
            pl.BlockSpec((ts, LANES), lambda i, j: (j, 0)),
            _const_spec((1, ATTN_WIDTH)),
            _const_spec((1, KV_WIDTH)),
            _const_spec((ATTN_WIDTH, ATTN_WIDTH)),
        ],
        out_specs=[
            pl.BlockSpec((1, ATTN_WIDTH, ts), lambda i, j: (i, 0, j)),
            pl.BlockSpec((1, ts, KV_WIDTH), lambda i, j: (i, j, 0)),
            pl.BlockSpec((1, N_KV_HEADS * VT_ROWS, ts), lambda i, j: (i, 0, j)),
        ],
        out_shape=[
            jax.ShapeDtypeStruct((b, ATTN_WIDTH, s), BF16),
            jax.ShapeDtypeStruct((b, s, KV_WIDTH), BF16),
            jax.ShapeDtypeStruct((b, N_KV_HEADS * VT_ROWS, s), BF16),
        ],
        compiler_params=_params(("parallel", "parallel")),
        name="qk_prep",
    )(qkv, qkv, qkv, cos_t, sin_t, qw, kw, seg_ones)


def _attn_setup(qt_ref, k_ref, bound_ref, qe_scr, shift_scr):
    g = pl.program_id(1)
    s_len = k_ref.shape[1]
    kc = min(KEY_CHUNK, s_len)
    qt = qt_ref[0]
    q_cols = jnp.concatenate([qt[r * HEAD_DIM:(r + 1) * HEAD_DIM, :] for r in range(REP)], axis=1)
    @pl.when(pl.program_id(2) == 0)
    def _():
        qe_scr[...] = jnp.zeros_like(qe_scr)

    qe_scr[pl.ds(pl.multiple_of(g * HEAD_DIM, HEAD_DIM), HEAD_DIM), :] = q_cols
    bound = bound_ref[0, 0]

    @pl.when(bound <= SAFE_SHIFT)
    def _():
        shift_scr[...] = jnp.full(shift_scr.shape, bound, F32)

    @pl.when(bound > SAFE_SHIFT)
    def _():
        def body(c, mx):
            r0 = pl.multiple_of(c * kc, kc)
            st = _dot(k_ref[0, pl.ds(r0, kc), :], qe_scr[...])
            return jnp.maximum(mx, jnp.max(st, axis=0, keepdims=True))

        shift_scr[...] = lax.fori_loop(0, s_len // kc, body, jnp.full(shift_scr.shape, -jnp.inf, F32))


def _attn_stages(k_ref, vt_ref, o_ref, qe_scr, shift_scr, p_scr):
    tq = o_ref.shape[1]
    s_len = k_ref.shape[1]
    kc = min(KEY_CHUNK, s_len)
    n_kc = s_len // kc
    n_cols = REP * tq
    acc = jnp.zeros((VT_ROWS, n_cols), F32)
    group = min(PV_CHUNKS, n_kc)
    scores = lambda c: _dot(k_ref[0, c * kc:(c + 1) * kc, :], qe_scr[...])
    st_next = scores(0)
    for c in range(n_kc + 1):
        st = st_next
        if c + 1 < n_kc:
            st_next = scores(c + 1)
        if c > 0 and c % group == 0:
            first = c - group
            acc = acc + _dot(vt_ref[0, :, first * kc:c * kc], p_scr[(first // group) % 2])
        yield
        if c < n_kc:
            p_scr[(c // group) % 2, (c % group) * kc:(c % group + 1) * kc, :] = (
                jnp.exp2(st - shift_scr[...]).astype(BF16))
            yield
    inv_denom = 1.0 / acc[HEAD_DIM:HEAD_DIM + 1]
    ot = (acc[:HEAD_DIM] * inv_denom).T
    out = jnp.concatenate([ot[r * tq:(r + 1) * tq, :] for r in range(REP)], axis=1)
    o_ref[0] = out.astype(o_ref.dtype)


def _split_hi_lo(v):
    hi = v.astype(BF16)
    lo = (v - hi.astype(F32)).astype(BF16)
    return hi, lo


def _softplus(v):
    return jnp.maximum(v, 0.0) + jnp.log1p(jnp.exp(-jnp.abs(v)))


def _mixer_kernel(bound_ref, qt_ref, k_ref, vt_ref,
                  xc_ref, b_ref, bt_ref, c_ref, dt_ref, dtt_ref,
                  dtb_ref, alog_ref, dtbt_ref, alogt_ref, dskip_ref,
                  tril_ref, triu_ref,
                  attn_ref, y_ref,
                  qe_scr, shift_scr, p_scr, y_scr, stf_scr, stb_scr):
    _attn_setup(qt_ref, k_ref, bound_ref, qe_scr, shift_scr)
    j = pl.program_id(2)
    n_blocks = pl.num_programs(2)
    s_len = xc_ref.shape[1]
    n_chunks = s_len // CHUNK
    iters = n_chunks // (s_len // qt_ref.shape[2])
    hp = HEADS_PER_GROUP
    rows = min(COPY_ROWS, s_len)

    @pl.when(j == 0)
    def _():
        def skip_body(i, carry):
            r0 = pl.multiple_of(i * rows, rows)
            y_scr[pl.ds(r0, rows), :] = dskip_ref[...] * xc_ref[0, pl.ds(r0, rows), :].astype(F32)
            return carry

        lax.fori_loop(0, s_len // rows, skip_body, 0)
        stf_scr[...] = jnp.zeros_like(stf_scr)
        stb_scr[...] = jnp.zeros_like(stb_scr)

    row = lax.broadcasted_iota(jnp.int32, (CHUNK, CHUNK), 0)
    col = lax.broadcasted_iota(jnp.int32, (CHUNK, CHUNK), 1)
    lane = lax.broadcasted_iota(jnp.int32, (CHUNK, LANES), 1)
    lane_lt_half = lane < SSM_HEAD_DIM
    neg_a = -jnp.exp(alog_ref[0]) * LOG2E
    neg_a_t = -jnp.exp(alogt_ref[0]) * LOG2E

    def chunk_step(forward, c, st_scr):
        tri = tril_ref[...] if forward else triu_ref[...]
        tri_t = triu_ref[...] if forward else tril_ref[...]
        keep = (row >= col) if forward else (row <= col)
        lane0 = 0 if forward else hp
        last = CHUNK - 1 if forward else 0
        r0 = pl.multiple_of(c * CHUNK, CHUNK)
        bc = b_ref[0, pl.ds(r0, CHUNK), :]
        bt = bt_ref[0, :, pl.ds(r0, CHUNK)]
        cc = c_ref[0, pl.ds(r0, CHUNK), :]
        dts = _softplus(dt_ref[0, pl.ds(r0, CHUNK), :] + dtb_ref[0])
        a_hi, a_lo = _split_hi_lo(dts * neg_a)
        a_cat = jnp.concatenate([a_hi, a_lo], axis=1)
        dts_t = _softplus(dtt_ref[0, 0, :, pl.ds(r0, CHUNK)] + dtbt_ref[0])
        at_hi, at_lo = _split_hi_lo(dts_t * neg_a_t)
        at_cat = jnp.concatenate([at_hi, at_lo], axis=0)
        yield
        acs2 = _dot(tri, a_cat)
        acst2 = _dot(at_cat, tri_t)
        cb = _dot_nt(cc, bc)
        yield
        acs = acs2[:, :LANES] + acs2[:, LANES:]
        acs_t = acst2[:DT_ROWS] + acst2[DT_ROWS:]
        ms, dt_tiles, acs_tiles = [], [], []
        for j in range(hp // 2):
            pair, dt_cols, acs_cols = [], [], []
            for h in (2 * j, 2 * j + 1):
                acs_col = jnp.broadcast_to(acs[:, lane0 + h:lane0 + h + 1], (CHUNK, LANES))
                dt_cols.append(jnp.broadcast_to(dts[:, lane0 + h:lane0 + h + 1], (CHUNK, LANES)))
                acs_cols.append(acs_col)
                seg = acs_col - acs_t[lane0 + h:lane0 + h + 1, :]
                lmat = jnp.exp2(jnp.where(keep, seg, -jnp.inf))
                pair.append((cb * lmat).astype(BF16))
            ms.append(jnp.concatenate(pair, axis=1))
            dt_tiles.append(jnp.where(lane_lt_half, dt_cols[0], dt_cols[1]))
            acs_tiles.append(jnp.where(lane_lt_half, acs_cols[0], acs_cols[1]))
        dt_exp = jnp.concatenate(dt_tiles, axis=1)
        acs_exp = jnp.concatenate(acs_tiles, axis=1)
        acs_last = acs_exp[last:last + 1, :]
        xdt = xc_ref[0, pl.ds(r0, CHUNK), :].astype(F32) * dt_exp
        xdt_b = xdt.astype(BF16)
        xd = (xdt * jnp.exp2(acs_last - acs_exp)).astype(BF16)
        rhs = []
        for j in range(hp // 2):
            xp = xdt_b[:, j * LANES:(j + 1) * LANES]
            zero = jnp.zeros_like(xp)
            rhs.append(jnp.concatenate([jnp.where(lane_lt_half, xp, zero),
                                        jnp.where(lane_lt_half, zero, xp)], axis=0))
        yield
        state = st_scr[...]
        y_off = _dot(cc, state.astype(BF16))
        st_new = _dot(bt, xd)
        pairs = [_dot(ms[j], rhs[j]) for j in range(hp // 2)]
        yield
        st_scr[...] = state * jnp.exp2(acs_last) + st_new
        y_scr[pl.ds(r0, CHUNK), :] += y_off * jnp.exp2(acs_exp) + jnp.concatenate(pairs, axis=1)

    def scan_stages():
        for it in range(iters):
            i = j * iters + it
            active = [chunk_step(True, i, stf_scr), chunk_step(False, n_chunks - 1 - i, stb_scr)]
            while active:
                active = [gen for gen in active if next(gen, True) is None]
                yield

    streams = [_attn_stages(k_ref, vt_ref, attn_ref, qe_scr, shift_scr, p_scr), scan_stages()]
    while streams:
        streams = [gen for gen in streams if next(gen, True) is None]

    @pl.when(j == n_blocks - 1)
    def _():
        def out_body(i, carry):
            r0 = pl.multiple_of(i * rows, rows)
            y_ref[0, pl.ds(r0, rows), :] = y_scr[pl.ds(r0, rows), :].astype(y_ref.dtype)
            return carry

        lax.fori_loop(0, s_len // rows, out_body, 0)


def _mixer(score_bound, qt, kn, vt, xc, bt, dt_pad, dt_t, dtb, alog, dtb_t, alog_t, dskip, tril, triu):
    b, s, _ = xc.shape
    tq = min(TQ, s)
    assert (s // CHUNK) % (s // tq) == 0
    grid = (b, N_SSM_GROUPS, s // tq)
    gw = GROUP_WIDTH
    aw = REP * HEAD_DIM
    b_block0 = D_INNER // D_STATE
    c_block0 = b_block0 + N_SSM_GROUPS
    once = pl.Buffered(1)
    seq = lambda width, off: pl.BlockSpec((1, s, width), lambda i, g, j: (i, 0, off + g), pipeline_mode=once)
    gspec = lambda rows: pl.BlockSpec((1, rows, LANES), lambda i, g, j: (g, 0, 0))
    return pl.pallas_call(
        _mixer_kernel,
        grid=grid,
        in_specs=[
            pl.BlockSpec(memory_space=pltpu.SMEM),
            pl.BlockSpec((1, aw, tq), lambda i, g, j: (i, g, j)),
            pl.BlockSpec((1, s, KV_WIDTH), lambda i, g, j: (i, 0, 0), pipeline_mode=once),
            pl.BlockSpec((1, VT_ROWS, s), lambda i, g, j: (i, g, 0)),
            seq(gw, 0),
            pl.BlockSpec((1, s, D_STATE), lambda i, g, j: (i, 0, b_block0 + g)),
            pl.BlockSpec((1, D_STATE, s), lambda i, g, j: (i, g, 0), pipeline_mode=once),
            pl.BlockSpec((1, s, D_STATE), lambda i, g, j: (i, 0, c_block0 + g)),
            seq(LANES, 0),
            pl.BlockSpec((1, 1, DT_ROWS, s), lambda i, g, j: (i, g, 0, 0)),
            gspec(1), gspec(1), gspec(DT_ROWS), gspec(DT_ROWS),
            pl.BlockSpec((1, gw), lambda i, g, j: (0, g)),
            _const_spec((CHUNK, CHUNK)), _const_spec((CHUNK, CHUNK)),
        ],
        out_specs=[
            pl.BlockSpec((1, tq, aw), lambda i, g, j: (i, j, g)),
            pl.BlockSpec((1, s, gw), lambda i, g, j: (i, 0, g), pipeline_mode=once),
        ],
        out_shape=[
            jax.ShapeDtypeStruct((b, s, ATTN_WIDTH), BF16),
            jax.ShapeDtypeStruct((b, s, D_INNER), BF16),
        ],
        scratch_shapes=[
            pltpu.VMEM((KV_WIDTH, REP * tq), BF16),
            pltpu.VMEM((1, REP * tq), F32),
            pltpu.VMEM((2, min(PV_CHUNKS * KEY_CHUNK, s), REP * tq), BF16),
            pltpu.VMEM((s, gw), F32),
            pltpu.VMEM((D_STATE, gw), F32),
            pltpu.VMEM((D_STATE, gw), F32),
        ],
        compiler_params=_params(("parallel", "arbitrary", "arbitrary")),
        name="mixer",
    )(score_bound, qt, kn, vt, xc, xc, bt, xc, dt_pad, dt_t, dtb, alog, dtb_t, alog_t, dskip, tril, triu)


def _outmlp_kernel(attn_ref, y_ref, z_ref, x_ref, nw_ref, woa_ref, wos_ref, ln2_ref, wup_ref, wdn_ref,
                   fin_ref, o_ref, *, final_norm):
    gated = []
    for gi in range(N_SSM_GROUPS):
        cols = slice(gi * GROUP_WIDTH, (gi + 1) * GROUP_WIDTH)
        yg = y_ref[0, :, cols].astype(F32) * _silu(z_ref[0, :, cols].astype(F32))
        msg = jnp.mean(yg * yg, axis=-1, keepdims=True)
        gated.append((yg * lax.rsqrt(msg + NORM_EPS) * nw_ref[:, cols]).astype(BF16))
    ssm = jnp.concatenate(gated, axis=1)
    x1 = x_ref[0] + _dot(attn_ref[0], woa_ref[...]) + _dot(ssm, wos_ref[...])
    ms = jnp.mean(x1 * x1, axis=-1, keepdims=True)
    h = (x1 * lax.rsqrt(ms + NORM_EPS) * ln2_ref[...]).astype(BF16)
    u = jnp.maximum(_dot(h, wup_ref[...]), 0.0)
    x2 = x1 + _dot((u * u).astype(BF16), wdn_ref[...])
    if final_norm:
        ms2 = jnp.mean(x2 * x2, axis=-1, keepdims=True)
        x2 = x2 * lax.rsqrt(ms2 + NORM_EPS) * fin_ref[...]
    o_ref[0] = x2


def _out_mlp(attn, y, z, x, norm_w, wo_attn, wo_ssm, ln2, w_up, w_down, fin_w, final_norm):
    b, s, _ = x.shape
    tm = min(TM_MLP, s)
    grid = (b, s // tm)
    tok = lambda width: pl.BlockSpec((1, tm, width), lambda i, j: (i, j, 0))
    return pl.pallas_call(
        functools.partial(_outmlp_kernel, final_norm=final_norm),
        grid=grid,
        in_specs=[
            tok(ATTN_WIDTH), tok(D_INNER), tok(D_INNER), tok(D_MODEL),
            _const_spec((1, D_INNER)),
            _const_spec((ATTN_WIDTH, D_MODEL)), _const_spec((D_INNER, D_MODEL)),
            _const_spec((1, D_MODEL)),
            _const_spec((D_MODEL, D_FF)), _const_spec((D_FF, D_MODEL)),
            _const_spec((1, D_MODEL)),
        ],
        out_specs=tok(D_MODEL),
        out_shape=jax.ShapeDtypeStruct((b, s, D_MODEL), F32),
        compiler_params=_params(("parallel", "parallel")),
        name="out_mlp",
    )(attn, y, z, x, norm_w, wo_attn, wo_ssm, ln2, w_up, w_down, fin_w)


def _rope_tables(seq):
    rows = seq // GRID_W
    row_ids = jnp.repeat(jnp.arange(rows, dtype=jnp.int32), GRID_W)
    col_ids = jnp.tile(jnp.arange(GRID_W, dtype=jnp.int32), rows)
    half = HEAD_DIM // 2
    inv_freq = ROPE_THETA ** (-jnp.arange(0, half, 2, dtype=F32) / half)

    def ang(pos):
        a = pos.astype(F32)[:, None] * inv_freq[None, :]
        return jnp.concatenate([a, a], axis=-1)

    a = jnp.concatenate([ang(row_ids), ang(col_ids)], axis=-1)
    sign = jnp.tile(jnp.concatenate([-jnp.ones((half // 2,), F32), jnp.ones((half // 2,), F32)]), 2)
    cos = jnp.cos(a)
    sin = jnp.sin(a) * sign[None, :]
    return jnp.tile(cos, (1, LANES // HEAD_DIM)), jnp.tile(sin, (1, LANES // HEAD_DIM))


def _constants():
    seg = np.kron(np.eye(N_Q_HEADS, dtype=np.float32), np.ones((HEAD_DIM, HEAD_DIM), np.float32))
    tril = np.tril(np.ones((CHUNK, CHUNK), np.float32))
    return dict(
        seg=jnp.asarray(seg, BF16),
        tril=jnp.asarray(tril, BF16), triu=jnp.asarray(tril.T, BF16),
    )


def _group_rows(vf, vb):
    return jnp.concatenate([vf.reshape(N_SSM_GROUPS, HEADS_PER_GROUP),
                            vb.reshape(N_SSM_GROUPS, HEADS_PER_GROUP)], axis=1)


def _layer_weights(w_in, dt_bias_f, dt_bias_b, a_log_f, a_log_b):
    wdt = w_in[:, XBC_OFF + CONV_DIM:]
    wf = wdt[:, :N_SSM_HEADS].reshape(D_MODEL, N_SSM_GROUPS, HEADS_PER_GROUP)
    wb = wdt[:, N_SSM_HEADS:].reshape(D_MODEL, N_SSM_GROUPS, HEADS_PER_GROUP)
    grp = jnp.concatenate([wf, wb], axis=-1)
    wdt_pad = jnp.pad(grp, ((0, 0), (0, 0), (0, LANES - DT_ROWS))).reshape(D_MODEL, DT_PAD)
    w_all = jnp.concatenate([w_in[:, :DT_OFF], wdt_pad], axis=1).astype(BF16)
    wdt_t = grp.reshape(D_MODEL, N_SSM_GROUPS * DT_ROWS).T.astype(BF16)
    pad_lanes = lambda v: jnp.pad(v, ((0, 0), (0, LANES - DT_ROWS)))[:, None, :]
    dtb = _group_rows(dt_bias_f, dt_bias_b)
    alog = _group_rows(a_log_f, a_log_b)
    bcast = lambda v: jnp.broadcast_to(v[:, :, None], (N_SSM_GROUPS, DT_ROWS, LANES))
    return w_all, wdt_t, pad_lanes(dtb), pad_lanes(alog), bcast(dtb), bcast(alog)


def kernel(x, ln1_w, w_in, conv_w, conv_b, dt_bias_fwd, dt_bias_bwd, a_log_fwd, a_log_bwd, d_skip,
           ssm_norm_w, q_norm_w, k_norm_w, w_out, ln2_w, w_up, w_down, final_norm_w):
    b, s, _ = x.shape
    depth = w_in.shape[0]
    consts = _constants()
    cos_t, sin_t = _rope_tables(s)
    row = lambda v: v.reshape(1, -1).astype(F32)
    for i in range(depth):
        w_all, wdt_t, dtb, alog, dtb_t, alog_t = _layer_weights(
            w_in[i], dt_bias_fwd[i], dt_bias_bwd[i], a_log_fwd[i], a_log_bwd[i])
        qkv, z, xc, bt, dt_pad, dt_t = _in_proj(
            x, row(ln1_w[i]), w_all, wdt_t, conv_w[i].astype(F32), row(conv_b[i]))
        qt, kn, vt = _qk_prep(
            qkv, cos_t, sin_t, row(jnp.tile(q_norm_w[i], N_Q_HEADS)), row(jnp.tile(k_norm_w[i], N_KV_HEADS)),
            consts["seg"])
        score_bound = (HEAD_DIM ** 0.5 * LOG2E * jnp.max(jnp.abs(q_norm_w[i])) * jnp.max(jnp.abs(k_norm_w[i]))
                       ).astype(F32).reshape(1, 1)
        attn, y = _mixer(
            score_bound, qt, kn, vt, xc, bt, dt_pad, dt_t.reshape(b, N_SSM_GROUPS, DT_ROWS, s), dtb, alog, dtb_t, alog_t,
            row(jnp.repeat(d_skip[i], SSM_HEAD_DIM)), consts["tril"], consts["triu"])
        x = _out_mlp(
            attn, y, z, x, row(ssm_norm_w[i]),
            w_out[i, :ATTN_WIDTH].astype(BF16), w_out[i, ATTN_WIDTH:].astype(BF16),
            row(ln2_w[i]), w_up[i].astype(BF16), w_down[i].astype(BF16), row(final_norm_w),
            final_norm=(i == depth - 1))
    return x
```
